```python
import jax, jax.numpy as jnp
from jax import lax
import numpy as np

D_MODEL = 1024
BATCH = 8
SEQ = 2048
DEPTH = 1
DEC_BATCH = 128
DEC_SEQ = 4
PAST_LEN = 16384
PAGE_SIZE = 128

N_META = 16
D_CONV = D_MODEL
CONV_A_WIDTH = 3
HEAD_K = 128
HEAD_V = 128
N_HEADS = D_MODEL // 128
QKV_DIM = 2 * N_HEADS * HEAD_K + N_HEADS * HEAD_V
CONV_QKV_WIDTH = 4
CHUNK = 64
EPS = 1e-6
IN_SIZES = (D_CONV, D_CONV, D_CONV, D_CONV, QKV_DIM, N_HEADS * HEAD_V, N_HEADS, N_HEADS, D_MODEL, D_MODEL)
N_IN = 4 * D_CONV + QKV_DIM + N_HEADS * HEAD_V + 2 * N_HEADS + 2 * D_MODEL

kernel_name = "hybrid_shortconv_gated_deltanet_step"


def _rmsnorm(x, w):
    xf = x.astype(jnp.float32)
    y = xf * lax.rsqrt(jnp.mean(xf * xf, axis=-1, keepdims=True) + EPS)
    return (y * w.astype(jnp.float32)).astype(x.dtype)


def _l2norm(x):
    return x * lax.rsqrt(jnp.sum(x * x, axis=-1, keepdims=True) + EPS)


def _causal_dwconv(x, prefix, w):
    width = w.shape[0]
    t_len = x.shape[1]
    xp = jnp.concatenate([prefix.astype(x.dtype), x], axis=1)
    y = xp[:, 0:t_len] * w[0]
    for j in range(1, width):
        y = y + xp[:, j:j + t_len] * w[j]
    return y, xp[:, t_len:]


def _delta_chunk(q, k, v, beta, g, s):
    L = q.shape[2]
    G = jnp.cumsum(g, axis=-1)
    diff = G[..., :, None] - G[..., None, :]
    incl = jnp.tril(jnp.ones((L, L), dtype=bool))
    strict = jnp.tril(jnp.ones((L, L), dtype=bool), -1)
    decay = jnp.exp(jnp.where(incl, diff, -jnp.inf))
    kb = k * beta[..., None]
    a = jnp.where(strict, jnp.einsum('bhid,bhjd->bhij', kb, k) * decay, 0.0)
    eye = jnp.eye(L, dtype=q.dtype)
    t_inv = lax.linalg.triangular_solve(eye + a, jnp.broadcast_to(eye, a.shape),
                                        left_side=True, lower=True, unit_diagonal=True)
    value = jnp.einsum('bhij,bhjv->bhiv', t_inv, v * beta[..., None])
    k_cum = jnp.einsum('bhij,bhjd->bhid', t_inv, kb * jnp.exp(G)[..., None])
    v_new = value - jnp.einsum('bhld,bhdv->bhlv', k_cum, s)
    attn = jnp.einsum('bhid,bhjd->bhij', q, k) * decay
    o = (jnp.einsum('bhld,bhdv->bhlv', q * jnp.exp(G)[..., None], s)
         + jnp.einsum('bhij,bhjv->bhiv', attn, v_new))
    g_last = G[..., -1]
    s_new = (s * jnp.exp(g_last)[..., None, None]
             + jnp.einsum('bhld,bhlv->bhdv', k * jnp.exp(g_last[..., None] - G)[..., None], v_new))
    return o, s_new


def _delta_sequence(q, k, v, beta, g, s):
    t_len = q.shape[2]
    n_full = t_len // CHUNK
    rem = t_len - n_full * CHUNK
    outs = []
    if n_full > 0:
        def to_chunks(a):
            a = a[:, :, :n_full * CHUNK]
            a = a.reshape(a.shape[:2] + (n_full, CHUNK) + a.shape[3:])
            return jnp.moveaxis(a, 2, 0)
        xs = tuple(to_chunks(a) for a in (q, k, v, beta, g))

        def step(carry, c):
            o_c, carry = _delta_chunk(c[0], c[1], c[2], c[3], c[4], carry)
            return carry, o_c
        s, o_all = lax.scan(step, s, xs)
        o_all = jnp.moveaxis(o_all, 0, 2)
        outs.append(o_all.reshape(o_all.shape[:2] + (n_full * CHUNK, o_all.shape[-1])))
    if rem > 0:
        st = n_full * CHUNK
        o_r, s = _delta_chunk(q[:, :, st:], k[:, :, st:], v[:, :, st:], beta[:, :, st:], g[:, :, st:], s)
        outs.append(o_r)
    o = outs[0] if len(outs) == 1 else jnp.concatenate(outs, axis=2)
    return o, s


def _layer(x, conv_a_prev, conv_qkv_prev, s_prev, n_meta, norm_pre, norm_post, w_in, b_gate,
           conv_a_w, conv_qkv_w, a_log, dt_bias, gnorm_w, w_a_out, w_b_out, w_o):
    f32 = jnp.float32
    bsz, t_len, _ = x.shape
    xn = _rmsnorm(x, norm_pre)
    proj = xn @ w_in
    splits = []
    acc = 0
    for sz in IN_SIZES[:-1]:
        acc += sz
        splits.append(acc)
    h_a, b_a, c_a, z_a, qkv, z_b, beta_l, alpha_l, gate_a, gate_b = jnp.split(proj, splits, axis=-1)

    conv_u, conv_a_new = _causal_dwconv(c_a * h_a, conv_a_prev, conv_a_w)
    y_a = b_a * conv_u * jax.nn.silu(z_a)

    qkv_c, conv_qkv_new = _causal_dwconv(qkv, conv_qkv_prev, conv_qkv_w)
    qkv_c = jax.nn.silu(qkv_c).astype(f32)
    nk = N_HEADS * HEAD_K

    def heads(a, d):
        return a.reshape(bsz, t_len, N_HEADS, d).transpose(0, 2, 1, 3)
    q = _l2norm(heads(qkv_c[..., :nk], HEAD_K)) * (HEAD_K ** -0.5)
    k = _l2norm(heads(qkv_c[..., nk:2 * nk], HEAD_K))
    v = heads(qkv_c[..., 2 * nk:], HEAD_V)
    beta = jax.nn.sigmoid(beta_l.astype(f32)).transpose(0, 2, 1)
    g = (-jnp.exp(a_log.astype(f32))
         * jax.nn.softplus(alpha_l.astype(f32) + dt_bias.astype(f32))).transpose(0, 2, 1)
    if n_meta > 0:
        o_m, s_mid = _delta_sequence(q[:, :, :n_meta], k[:, :, :n_meta], v[:, :, :n_meta],
                                     beta[:, :, :n_meta], g[:, :, :n_meta], s_prev)
        o_r, s_new = _delta_sequence(q[:, :, n_meta:], k[:, :, n_meta:], v[:, :, n_meta:],
                                     beta[:, :, n_meta:], g[:, :, n_meta:], s_mid)
        o = jnp.concatenate([o_m, o_r], axis=2)
    else:
        o, s_new = _delta_sequence(q, k, v, beta, g, s_prev)
    o = o.transpose(0, 2, 1, 3).astype(x.dtype)
    o = _rmsnorm(o, gnorm_w) * jax.nn.silu(z_b.reshape(bsz, t_len, N_HEADS, HEAD_V))
    y_b = o.reshape(bsz, t_len, N_HEADS * HEAD_V)

    merged = (jax.nn.sigmoid(gate_a + b_gate[:D_MODEL]) * (y_a @ w_a_out)
              + jax.nn.sigmoid(gate_b + b_gate[D_MODEL:]) * (y_b @ w_b_out))
    x = x + _rmsnorm(merged @ w_o, norm_post)
    return x, conv_a_new, conv_qkv_new, s_new.astype(x.dtype)


def setup_inputs(seed: int = 0) -> dict:
    key = jax.random.key(seed)
    ks = jax.random.split(key, 18)
    f32 = jnp.float32

    def nrm(k, shape, scale):
        return jax.random.normal(k, shape, f32) * scale
    x_prompt = nrm(ks[0], (BATCH, SEQ, D_MODEL), 1.0)
    x_sample = nrm(ks[1], (DEC_BATCH, DEC_SEQ, D_MODEL), 1.0)
    state_conv_a = nrm(ks[2], (DEPTH, DEC_BATCH, CONV_A_WIDTH - 1, D_CONV), 1.0)
    state_conv_qkv = nrm(ks[3], (DEPTH, DEC_BATCH, CONV_QKV_WIDTH - 1, QKV_DIM), 1.0)
    state_delta = nrm(ks[4], (DEPTH, DEC_BATCH, N_HEADS, HEAD_K, HEAD_V), HEAD_K ** -0.5)
    meta = nrm(ks[5], (N_META, D_MODEL), 1.0)
    norm_pre = 1.0 + nrm(ks[6], (DEPTH, D_MODEL), 0.02)
    norm_post = 1.0 + nrm(ks[7], (DEPTH, D_MODEL), 0.02)
    w_in = nrm(ks[8], (DEPTH, D_MODEL, N_IN), D_MODEL ** -0.5)
    b_gate = nrm(ks[9], (DEPTH, 2 * D_MODEL), 0.01)
    conv_a_w = nrm(ks[10], (DEPTH, CONV_A_WIDTH, D_CONV), CONV_A_WIDTH ** -0.5)
    conv_qkv_w = nrm(ks[11], (DEPTH, CONV_QKV_WIDTH, QKV_DIM), CONV_QKV_WIDTH ** -0.5)
    a_log = jnp.log(jax.random.uniform(ks[12], (DEPTH, N_HEADS), f32, 1.0, 16.0))
    dt = jax.random.uniform(ks[13], (DEPTH, N_HEADS), f32, 1e-3, 1e-1)
    dt_bias = jnp.log(jnp.expm1(dt))
    gnorm_w = 1.0 + nrm(ks[14], (DEPTH, HEAD_V), 0.02)
    w_a_out = nrm(ks[15], (DEPTH, D_CONV, D_MODEL), D_CONV ** -0.5)
    w_b_out = nrm(ks[16], (DEPTH, N_HEADS * HEAD_V, D_MODEL), (N_HEADS * HEAD_V) ** -0.5)
    w_o = nrm(ks[17], (DEPTH, D_MODEL, D_MODEL), D_MODEL ** -0.5)
    return {"x_prompt": x_prompt, "x_sample": x_sample,
            "state_conv_a": state_conv_a, "state_conv_qkv": state_conv_qkv, "state_delta": state_delta,
            "meta": meta, "norm_pre": norm_pre, "norm_post": norm_post, "w_in": w_in, "b_gate": b_gate,
            "conv_a_w": conv_a_w, "conv_qkv_w": conv_qkv_w, "a_log": a_log, "dt_bias": dt_bias,
            "gnorm_w": gnorm_w, "w_a_out": w_a_out, "w_b_out": w_b_out, "w_o": w_o}


def reference(x_prompt, x_sample, state_conv_a, state_conv_qkv, state_delta, meta, norm_pre, norm_post,
              w_in, b_gate, conv_a_w, conv_qkv_w, a_log, dt_bias, gnorm_w, w_a_out, w_b_out, w_o):
    dtype = x_prompt.dtype
    bsz = x_prompt.shape[0]
    xp = jnp.concatenate([jnp.broadcast_to(meta[None].astype(dtype), (bsz, N_META, D_MODEL)), x_prompt], axis=1)
    xs = x_sample
    ca_p, cq_p, sd_p, ca_s, cq_s, sd_s = [], [], [], [], [], []
    for l in range(DEPTH):
        w = (norm_pre[l], norm_post[l], w_in[l], b_gate[l], conv_a_w[l], conv_qkv_w[l],
             a_log[l], dt_bias[l], gnorm_w[l], w_a_out[l], w_b_out[l], w_o[l])
        zero_a = jnp.zeros((bsz, CONV_A_WIDTH - 1, D_CONV), dtype)
        zero_q = jnp.zeros((bsz, CONV_QKV_WIDTH - 1, QKV_DIM), dtype)
        zero_s = jnp.zeros((bsz, N_HEADS, HEAD_K, HEAD_V), jnp.float32)
        xp, ca, cq, sd = _layer(xp, zero_a, zero_q, zero_s, N_META, *w)
        ca_p.append(ca)
        cq_p.append(cq)
        sd_p.append(sd)
        xs, ca, cq, sd = _layer(xs, state_conv_a[l], state_conv_qkv[l],
                                state_delta[l].astype(jnp.float32), 0, *w)
        ca_s.append(ca)
        cq_s.append(cq)
        sd_s.append(sd)
    y_prompt = xp[:, N_META:]
    return (y_prompt, xs, jnp.stack(ca_p), jnp.stack(cq_p), jnp.stack(sd_p),
            jnp.stack(ca_s), jnp.stack(cq_s), jnp.stack(sd_s))
```

```python
import functools
import math

import jax
import jax.numpy as jnp
from jax import lax
from jax.experimental import pallas as pl
from jax.experimental.pallas import tpu as pltpu

HEAD = 128
CONV_A_WIDTH = 3
CONV_QKV_WIDTH = 4
CHUNK = 64
N_META = 16
EPS = 1e-6
SUBLANES = 8
BF16 = jnp.bfloat16
F32 = jnp.float32


def _dot(a, b):
    return jnp.dot(a.astype(BF16), b.astype(BF16), preferred_element_type=F32)


def _dot_nt(a, b):
    return lax.dot_general(a.astype(BF16), b.astype(BF16), (((1,), (1,)), ((), ())),
                           preferred_element_type=F32)


def _dot_tn(a, b):
    return lax.dot_general(a.astype(BF16), b.astype(BF16), (((0,), (0,)), ((), ())),
                           preferred_element_type=F32)


def _sigmoid(x):
    return 1.0 / (1.0 + jnp.exp(-x))


def _silu(x):
    return x * _sigmoid(x)


def _softplus(x):
    return jnp.maximum(x, 0.0) + jnp.log1p(jnp.exp(-jnp.abs(x)))


def _rms(x, w):
    return x * lax.rsqrt(jnp.mean(x * x, axis=-1, keepdims=True) + EPS) * w


def _cumsum_rows(x):
    n = x.shape[0]
    row = lax.broadcasted_iota(jnp.int32, x.shape, 0)
    sh = 1
    while sh < n:
        x = x + jnp.where(row >= sh, pltpu.roll(x, sh, 0), 0.0)
        sh *= 2
    return x


def _unit_lower_inverse(a_strict):
    n = a_strict.shape[0]
    ii = lax.broadcasted_iota(jnp.int32, (n, n), 0)
    jj = lax.broadcasted_iota(jnp.int32, (n, n), 1)
    eye = jnp.where(ii == jj, 1.0, 0.0).astype(F32)
    m = -a_strict
    p = eye + m
    for _ in range(int(math.log2(n)) - 1):
        m = _dot(m, m)
        p = p + _dot(p, m)
    return p


def _delta_chunk(q, k, v, beta, g_cum, g_cum_row, s):
    n = q.shape[0]
    ii = lax.broadcasted_iota(jnp.int32, (n, n), 0)
    jj = lax.broadcasted_iota(jnp.int32, (n, n), 1)
    incl = ii >= jj
    diff = g_cum - g_cum_row
    decay = jnp.where(incl, jnp.exp(jnp.where(incl, diff, 0.0)), 0.0)
    kb = k * beta
    a = jnp.where(ii > jj, _dot_nt(kb, k) * decay, 0.0)
    t_inv = _unit_lower_inverse(a)
    e_g = jnp.exp(g_cum)
    x = _dot(t_inv, jnp.concatenate([v * beta, kb * e_g], axis=1))
    value, k_cum = x[:, :HEAD], x[:, HEAD:]
    r = _dot(jnp.concatenate([k_cum, q * e_g], axis=0), s)
    v_new = value - r[:n]
    attn = _dot_nt(q, k) * decay
    o = r[n:] + _dot(attn, v_new)
    g_last = g_cum[n - 1:n, :]
    s_new = s * jnp.exp(g_last) + _dot_tn(k * jnp.exp(g_last - g_cum), v_new)
    return o, s_new


def _seq_kernel(x_ref, ca0_ref, cq0_ref, s0_ref, npre_ref, npost_ref, wmain_ref, wab_ref, bgate_ref,
                caw_ref, cqw_ref, alog_ref, dtb_ref, gnw_ref, wa_ref, wb_ref, wo_ref,
                y_ref, ca_ref, cq_ref, s_ref,
                ubuf, qbuf, qkv_scr, o_scr, *, chunk):
    tt, d = x_ref.shape[1], x_ref.shape[2]
    n_heads = d // HEAD
    t_idx = pl.program_id(1)

    @pl.when(t_idx == 0)
    def _():
        ubuf[0:SUBLANES, :] = ca0_ref[0]
        qbuf[0:SUBLANES, :] = cq0_ref[0]
        s_ref[0] = s0_ref[0]

    @pl.when(t_idx > 0)
    def _():
        ubuf[0:SUBLANES, :] = ubuf[tt:tt + SUBLANES, :]
        qbuf[0:SUBLANES, :] = qbuf[tt:tt + SUBLANES, :]

    x = x_ref[0]
    xn = _rms(x, npre_ref[...]).astype(BF16)

    pa = jnp.dot(xn, wmain_ref[:, 0:4 * d], preferred_element_type=F32)
    ubuf[SUBLANES:SUBLANES + tt, :] = pa[:, 2 * d:3 * d] * pa[:, 0:d]
    conv_u = ubuf[SUBLANES:SUBLANES + tt, :] * caw_ref[CONV_A_WIDTH - 1:CONV_A_WIDTH, :]
    for j in range(CONV_A_WIDTH - 1):
        off = SUBLANES - (CONV_A_WIDTH - 1) + j
        conv_u = conv_u + ubuf[off:off + tt, :] * caw_ref[j:j + 1, :]
    y_a = pa[:, d:2 * d] * conv_u * _silu(pa[:, 3 * d:4 * d])
    gate_a = _sigmoid(jnp.dot(xn, wmain_ref[:, 8 * d:9 * d], preferred_element_type=F32) + bgate_ref[:, 0:d])
    merged = gate_a * _dot(y_a, wa_ref[...])
    ca_ref[0] = ubuf[tt:tt + SUBLANES, :]

    qbuf[SUBLANES:SUBLANES + tt, :] = jnp.dot(xn, wmain_ref[:, 4 * d:7 * d], preferred_element_type=F32)
    conv_q = qbuf[SUBLANES:SUBLANES + tt, :] * cqw_ref[CONV_QKV_WIDTH - 1:CONV_QKV_WIDTH, :]
    for j in range(CONV_QKV_WIDTH - 1):
        off = SUBLANES - (CONV_QKV_WIDTH - 1) + j
        conv_q = conv_q + qbuf[off:off + tt, :] * cqw_ref[j:j + 1, :]
    qkv_scr[...] = _silu(conv_q)
    cq_ref[0] = qbuf[tt:tt + SUBLANES, :]

    ab = jnp.dot(xn, wab_ref[...], preferred_element_type=F32)
    beta_all = _sigmoid(ab)
    g_all = -jnp.exp(alog_ref[...]) * _softplus(ab + dtb_ref[...])

    for c in range(tt // chunk):
        rows = slice(c * chunk, (c + 1) * chunk)
        g_cum = _cumsum_rows(g_all[rows, :])
        g_cum_t = g_cum.T
        for h in range(n_heads):
            lanes = slice(h * HEAD, (h + 1) * HEAD)
            q = qkv_scr[rows, h * HEAD:(h + 1) * HEAD]
            k = qkv_scr[rows, d + h * HEAD:d + (h + 1) * HEAD]
            v = qkv_scr[rows, 2 * d + h * HEAD:2 * d + (h + 1) * HEAD]
            q = q * (lax.rsqrt(jnp.sum(q * q, axis=-1, keepdims=True) + EPS) * (HEAD ** -0.5))
            k = k * lax.rsqrt(jnp.sum(k * k, axis=-1, keepdims=True) + EPS)
            o, s_new = _delta_chunk(q, k, v, beta_all[rows, h:h + 1],
                                    g_cum[:, n_heads + h:n_heads + h + 1],
                                    g_cum_t[n_heads + h:n_heads + h + 1, :], s_ref[0, h])
            s_ref[0, h] = s_new
            o_scr[rows, lanes] = _rms(o, gnw_ref[...])

    z_b = jnp.dot(xn, wmain_ref[:, 7 * d:8 * d], preferred_element_type=F32)
    y_b = o_scr[...] * _silu(z_b)
    gate_b = _sigmoid(jnp.dot(xn, wmain_ref[:, 9 * d:10 * d], preferred_element_type=F32) + bgate_ref[:, d:2 * d])
    merged = merged + gate_b * _dot(y_b, wb_ref[...])
    y_ref[0] = x + _rms(_dot(merged, wo_ref[...]), npost_ref[...])


def _const_spec(shape):
    nd = len(shape)
    return pl.BlockSpec(shape, lambda b, t: (0,) * nd)


def _seq_layer(x, ca0, cq0, s0, weights, *, tile, chunk, shared_state):
    n, t_len, d = x.shape
    n_heads = d // HEAD
    state_map = (lambda b, t: (0, 0, 0)) if shared_state else (lambda b, t: (b, 0, 0))
    s_map = (lambda b, t: (0, 0, 0, 0)) if shared_state else (lambda b, t: (b, 0, 0, 0))
    in_specs = [
        pl.BlockSpec((1, tile, d), lambda b, t: (b, t, 0)),
        pl.BlockSpec((1, SUBLANES, d), state_map),
        pl.BlockSpec((1, SUBLANES, 3 * d), state_map),
        pl.BlockSpec((1, n_heads, HEAD, HEAD), s_map),
    ] + [_const_spec(w.shape) for w in weights]
    out_shape = (
        jax.ShapeDtypeStruct((n, t_len, d), F32),
        jax.ShapeDtypeStruct((n, SUBLANES, d), F32),
        jax.ShapeDtypeStruct((n, SUBLANES, 3 * d), F32),
        jax.ShapeDtypeStruct((n, n_heads, HEAD, HEAD), F32),
    )
    out_specs = (
        pl.BlockSpec((1, tile, d), lambda b, t: (b, t, 0)),
        pl.BlockSpec((1, SUBLANES, d), lambda b, t: (b, 0, 0)),
        pl.BlockSpec((1, SUBLANES, 3 * d), lambda b, t: (b, 0, 0)),
        pl.BlockSpec((1, n_heads, HEAD, HEAD), lambda b, t: (b, 0, 0, 0)),
    )
    scratch = [
        pltpu.VMEM((SUBLANES + tile, d), F32),
        pltpu.VMEM((SUBLANES + tile, 3 * d), F32),
        pltpu.VMEM((tile, 3 * d), F32),
        pltpu.VMEM((tile, d), F32),
    ]
    return pl.pallas_call(
        functools.partial(_seq_kernel, chunk=chunk),
        grid=(n, t_len // tile),
        in_specs=in_specs, out_specs=out_specs, out_shape=out_shape, scratch_shapes=scratch,
        compiler_params=pltpu.CompilerParams(dimension_semantics=("arbitrary", "arbitrary"),
                                             vmem_limit_bytes=56 * 1024 * 1024),
    )(x, ca0, cq0, s0, *weights)


def _tail_rows(state, width):
    n, r, c = state.shape
    return jnp.concatenate([jnp.zeros((n, SUBLANES - r, c), state.dtype), state], axis=1)


def _lane_row(vec, offset):
    return jnp.zeros((1, HEAD), F32).at[0, offset:offset + vec.shape[0]].set(vec.astype(F32))


def kernel(x_prompt, x_sample, state_conv_a, state_conv_qkv, state_delta, meta, norm_pre, norm_post, w_in, b_gate,
           conv_a_w, conv_qkv_w, a_log, dt_bias, gnorm_w, w_a_out, w_b_out, w_o):
    depth = w_in.shape[0]
    assert depth == 1, "single-layer trunk"
    d = x_prompt.shape[-1]
    n_heads = d // HEAD
    bsz = x_prompt.shape[0]
    l = 0
    w = w_in[l]
    o_small = 8 * d
    w_main = jnp.concatenate([w[:, :o_small], w[:, o_small + 2 * n_heads:]], axis=1).astype(BF16)
    w_ab = jnp.concatenate([w[:, o_small:o_small + 2 * n_heads],
                            jnp.zeros((d, HEAD - 2 * n_heads), w.dtype)], axis=1).astype(BF16)
    weights = (
        norm_pre[l][None, :], norm_post[l][None, :], w_main, w_ab, b_gate[l][None, :],
        conv_a_w[l], conv_qkv_w[l], _lane_row(a_log[l], n_heads), _lane_row(dt_bias[l], n_heads),
        gnorm_w[l][None, :], w_a_out[l].astype(BF16), w_b_out[l].astype(BF16), w_o[l].astype(BF16),
    )

    zeros_a = jnp.zeros((1, SUBLANES, d), F32)
    zeros_q = jnp.zeros((1, SUBLANES, 3 * d), F32)
    zeros_s = jnp.zeros((1, n_heads, HEAD, HEAD), F32)
    _, ca_m, cq_m, s_m = _seq_layer(meta[None].astype(F32), zeros_a, zeros_q, zeros_s, weights,
                                    tile=N_META, chunk=N_META, shared_state=True)
    tile = min(256, x_prompt.shape[1])
    y_p, ca_p, cq_p, s_p = _seq_layer(x_prompt, ca_m, cq_m, s_m, weights,
                                      tile=tile, chunk=CHUNK, shared_state=True)
    t_s = x_sample.shape[1]
    y_s, ca_s, cq_s, s_s = _seq_layer(x_sample, _tail_rows(state_conv_a[l], CONV_A_WIDTH),
                                      _tail_rows(state_conv_qkv[l], CONV_QKV_WIDTH), state_delta[l], weights,
                                      tile=t_s, chunk=t_s, shared_state=False)

    def tails(c, width):
        return c[None, :, SUBLANES - (width - 1):, :]

    return (y_p, y_s, tails(ca_p, CONV_A_WIDTH), tails(cq_p, CONV_QKV_WIDTH), s_p[None],
            tails(ca_s, CONV_A_WIDTH), tails(cq_s, CONV_QKV_WIDTH), s_s[None])
```

```python
import functools
import math

import jax
import jax.numpy as jnp
from jax import lax
from jax.experimental import pallas as pl
from jax.experimental.pallas import tpu as pltpu

HEAD = 128
CONV_A_WIDTH = 3
CONV_QKV_WIDTH = 4
CHUNK = 64
N_META = 16
EPS = 1e-6
SUBLANES = 8
BF16 = jnp.bfloat16
F32 = jnp.float32


def _dot(a, b):
    return jnp.dot(a.astype(BF16), b.astype(BF16), preferred_element_type=F32)


def _dot_nt(a, b):
    return lax.dot_general(a.astype(BF16), b.astype(BF16), (((1,), (1,)), ((), ())),
                           preferred_element_type=F32)


def _dot_tn(a, b):
    return lax.dot_general(a.astype(BF16), b.astype(BF16), (((0,), (0,)), ((), ())),
                           preferred_element_type=F32)


def _sigmoid(x):
    return 1.0 / (1.0 + jnp.exp(-x))


def _silu(x):
    return x * _sigmoid(x)


def _softplus(x):
    return jnp.maximum(x, 0.0) + jnp.log1p(jnp.exp(-jnp.abs(x)))


def _rms(x, w):
    return x * lax.rsqrt(jnp.mean(x * x, axis=-1, keepdims=True) + EPS) * w


def _chunk_cumsum(x, chunk):
    row = lax.broadcasted_iota(jnp.int32, x.shape, 0) & (chunk - 1)
    sh = 1
    while sh < chunk:
        x = x + jnp.where(row >= sh, pltpu.roll(x, sh, 0), 0.0)
        sh *= 2
    return x


def _block_diag(cat, blk_mask):
    n = cat.shape[1] // cat.shape[0]
    if n == 1:
        return cat
    return jnp.where(blk_mask, jnp.concatenate([cat] * n, axis=0), 0.0)


def _diag_blocks(bd, chunk):
    out = bd[0:chunk]
    for c in range(1, bd.shape[0] // chunk):
        out = out + bd[c * chunk:(c + 1) * chunk]
    return out


def _delta_tile(qkv_scr, beta_all, g_all, s_ref, o_scr, gnw, *, chunk):
    tt = qkv_scr.shape[0]
    d = qkv_scr.shape[1] // 3
    n_heads = d // HEAD
    n_c = tt // chunk
    shift = int(math.log2(chunk))
    ii = lax.broadcasted_iota(jnp.int32, (tt, tt), 0)
    jj = lax.broadcasted_iota(jnp.int32, (tt, tt), 1)
    blk = (ii >> shift) == (jj >> shift)
    incl = blk & (ii >= jj)
    strict = blk & (ii > jj)
    ci = lax.broadcasted_iota(jnp.int32, (chunk, tt), 0)
    cj = lax.broadcasted_iota(jnp.int32, (chunk, tt), 1) & (chunk - 1)
    eye_cat = jnp.where(ci == cj, 1.0, 0.0).astype(F32)

    g_cum_all = _chunk_cumsum(g_all, chunk)
    g_cum_t = g_cum_all.T
    e_g_all = jnp.exp(g_cum_all)

    qs, ks, vbs, kbes, gcs, attns, ms, ps = [], [], [], [], [], [], [], []
    for h in range(n_heads):
        q = qkv_scr[:, h * HEAD:(h + 1) * HEAD]
        k = qkv_scr[:, d + h * HEAD:d + (h + 1) * HEAD]
        v = qkv_scr[:, 2 * d + h * HEAD:2 * d + (h + 1) * HEAD]
        q = q * (lax.rsqrt(jnp.sum(q * q, axis=-1, keepdims=True) + EPS) * (HEAD ** -0.5))
        k = k * lax.rsqrt(jnp.sum(k * k, axis=-1, keepdims=True) + EPS)
        beta = beta_all[:, h:h + 1]
        g_cum = g_cum_all[:, n_heads + h:n_heads + h + 1]
        e_g = e_g_all[:, n_heads + h:n_heads + h + 1]
        diff = g_cum - g_cum_t[n_heads + h:n_heads + h + 1, :]
        decay = jnp.where(incl, jnp.exp(jnp.where(incl, diff, 0.0)), 0.0)
        kb = k * beta
        a_bd = jnp.where(strict, _dot_nt(kb, k) * decay, 0.0)
        attns.append(_dot_nt(q, k) * decay)
        m = -_diag_blocks(a_bd, chunk)
        ms.append(m)
        ps.append(eye_cat + m)
        qs.append(q * e_g)
        ks.append(k)
        vbs.append(v * beta)
        kbes.append(kb * e_g)
        gcs.append(g_cum)

    for _ in range(shift - 1):
        ms = [_dot(m, _block_diag(m, blk)) for m in ms]
        ps = [p + _dot(p, _block_diag(m, blk)) for p, m in zip(ps, ms)]
    xs = [_dot(_block_diag(p, blk), jnp.concatenate([vb, kbe], axis=1)) for p, vb, kbe in zip(ps, vbs, kbes)]

    v_news = [[] for _ in range(n_heads)]
    o_parts = [[] for _ in range(n_heads)]
    for c in range(n_c):
        rows = slice(c * chunk, (c + 1) * chunk)
        rs = [_dot(jnp.concatenate([xs[h][rows, HEAD:], qs[h][rows]], axis=0), s_ref[0, h])
              for h in range(n_heads)]
        for h in range(n_heads):
            v_new = xs[h][rows, :HEAD] - rs[h][:chunk]
            g_cum = gcs[h][rows]
            g_last = g_cum[chunk - 1:chunk, :]
            s_ref[0, h] = s_ref[0, h] * jnp.exp(g_last) + _dot_tn(ks[h][rows] * jnp.exp(g_last - g_cum), v_new)
            v_news[h].append(v_new)
            o_parts[h].append(rs[h][chunk:])
    for h in range(n_heads):
        o = jnp.concatenate(o_parts[h], axis=0) + _dot(attns[h], jnp.concatenate(v_news[h], axis=0))
        o_scr[:, h * HEAD:(h + 1) * HEAD] = _rms(o, gnw)


def _seq_kernel(x_ref, ca0_ref, cq0_ref, s0_ref, npre_ref, npost_ref, wmain_ref, wab_ref, bgate_ref,
                caw_ref, cqw_ref, alog_ref, dtb_ref, gnw_ref, wa_ref, wb_ref, wo_ref,
                y_ref, ca_ref, cq_ref, s_ref,
                ubuf, qbuf, qkv_scr, o_scr, *, chunk):
    tt, d = x_ref.shape[1], x_ref.shape[2]
    n_heads = d // HEAD
    t_idx = pl.program_id(1)

    @pl.when(t_idx == 0)
    def _():
        ubuf[0:SUBLANES, :] = ca0_ref[0]
        qbuf[0:SUBLANES, :] = cq0_ref[0]
        s_ref[0] = s0_ref[0]

    @pl.when(t_idx > 0)
    def _():
        ubuf[0:SUBLANES, :] = ubuf[tt:tt + SUBLANES, :]
        qbuf[0:SUBLANES, :] = qbuf[tt:tt + SUBLANES, :]

    x = x_ref[0]
    xn = _rms(x, npre_ref[...]).astype(BF16)

    pa = jnp.dot(xn, wmain_ref[:, 0:4 * d], preferred_element_type=F32)
    ubuf[SUBLANES:SUBLANES + tt, :] = pa[:, 2 * d:3 * d] * pa[:, 0:d]
    conv_u = ubuf[SUBLANES:SUBLANES + tt, :] * caw_ref[CONV_A_WIDTH - 1:CONV_A_WIDTH, :]
    for j in range(CONV_A_WIDTH - 1):
        off = SUBLANES - (CONV_A_WIDTH - 1) + j
        conv_u = conv_u + ubuf[off:off + tt, :] * caw_ref[j:j + 1, :]
    y_a = pa[:, d:2 * d] * conv_u * _silu(pa[:, 3 * d:4 * d])
    gate_a = _sigmoid(jnp.dot(xn, wmain_ref[:, 8 * d:9 * d], preferred_element_type=F32) + bgate_ref[:, 0:d])
    merged = gate_a * _dot(y_a, wa_ref[...])
    ca_ref[0] = ubuf[tt:tt + SUBLANES, :]

    qbuf[SUBLANES:SUBLANES + tt, :] = jnp.dot(xn, wmain_ref[:, 4 * d:7 * d], preferred_element_type=F32)
    conv_q = qbuf[SUBLANES:SUBLANES + tt, :] * cqw_ref[CONV_QKV_WIDTH - 1:CONV_QKV_WIDTH, :]
    for j in range(CONV_QKV_WIDTH - 1):
        off = SUBLANES - (CONV_QKV_WIDTH - 1) + j
        conv_q = conv_q + qbuf[off:off + tt, :] * cqw_ref[j:j + 1, :]
    qkv_scr[...] = _silu(conv_q)
    cq_ref[0] = qbuf[tt:tt + SUBLANES, :]

    ab = jnp.dot(xn, wab_ref[...], preferred_element_type=F32)
    beta_all = _sigmoid(ab)
    g_all = -jnp.exp(alog_ref[...]) * _softplus(ab + dtb_ref[...])

    _delta_tile(qkv_scr, beta_all, g_all, s_ref, o_scr, gnw_ref[...], chunk=chunk)

    z_b = jnp.dot(xn, wmain_ref[:, 7 * d:8 * d], preferred_element_type=F32)
    y_b = o_scr[...] * _silu(z_b)
    gate_b = _sigmoid(jnp.dot(xn, wmain_ref[:, 9 * d:10 * d], preferred_element_type=F32) + bgate_ref[:, d:2 * d])
    merged = merged + gate_b * _dot(y_b, wb_ref[...])
    y_ref[0] = x + _rms(_dot(merged, wo_ref[...]), npost_ref[...])


def _const_spec(shape):
    nd = len(shape)
    return pl.BlockSpec(shape, lambda b, t: (0,) * nd)


def _seq_layer(x, ca0, cq0, s0, weights, *, tile, chunk, shared_state):
    n, t_len, d = x.shape
    n_heads = d // HEAD
    state_map = (lambda b, t: (0, 0, 0)) if shared_state else (lambda b, t: (b, 0, 0))
    s_map = (lambda b, t: (0, 0, 0, 0)) if shared_state else (lambda b, t: (b, 0, 0, 0))
    in_specs = [
        pl.BlockSpec((1, tile, d), lambda b, t: (b, t, 0)),
        pl.BlockSpec((1, SUBLANES, d), state_map),
        pl.BlockSpec((1, SUBLANES, 3 * d), state_map),
        pl.BlockSpec((1, n_heads, HEAD, HEAD), s_map),
    ] + [_const_spec(w.shape) for w in weights]
    out_shape = (
        jax.ShapeDtypeStruct((n, t_len, d), F32),
        jax.ShapeDtypeStruct((n, SUBLANES, d), F32),
        jax.ShapeDtypeStruct((n, SUBLANES, 3 * d), F32),
        jax.ShapeDtypeStruct((n, n_heads, HEAD, HEAD), F32),
    )
    out_specs = (
        pl.BlockSpec((1, tile, d), lambda b, t: (b, t, 0)),
        pl.BlockSpec((1, SUBLANES, d), lambda b, t: (b, 0, 0)),
        pl.BlockSpec((1, SUBLANES, 3 * d), lambda b, t: (b, 0, 0)),
        pl.BlockSpec((1, n_heads, HEAD, HEAD), lambda b, t: (b, 0, 0, 0)),
    )
    scratch = [
        pltpu.VMEM((SUBLANES + tile, d), F32),
        pltpu.VMEM((SUBLANES + tile, 3 * d), F32),
        pltpu.VMEM((tile, 3 * d), F32),
        pltpu.VMEM((tile, d), F32),
    ]
    return pl.pallas_call(
        functools.partial(_seq_kernel, chunk=chunk),
        grid=(n, t_len // tile),
        in_specs=in_specs, out_specs=out_specs, out_shape=out_shape, scratch_shapes=scratch,
        compiler_params=pltpu.CompilerParams(dimension_semantics=("arbitrary", "arbitrary"),
                                             vmem_limit_bytes=56 * 1024 * 1024),
    )(x, ca0, cq0, s0, *weights)


def _tail_rows(state, width):
    n, r, c = state.shape
    return jnp.concatenate([jnp.zeros((n, SUBLANES - r, c), state.dtype), state], axis=1)


def _lane_row(vec, offset):
    return jnp.zeros((1, HEAD), F32).at[0, offset:offset + vec.shape[0]].set(vec.astype(F32))


def kernel(x_prompt, x_sample, state_conv_a, state_conv_qkv, state_delta, meta, norm_pre, norm_post, w_in, b_gate,
           conv_a_w, conv_qkv_w, a_log, dt_bias, gnorm_w, w_a_out, w_b_out, w_o):
    depth = w_in.shape[0]
    assert depth == 1, "single-layer trunk"
    d = x_prompt.shape[-1]
    n_heads = d // HEAD
    bsz = x_prompt.shape[0]
    l = 0
    w = w_in[l]
    o_small = 8 * d
    w_main = jnp.concatenate([w[:, :o_small], w[:, o_small + 2 * n_heads:]], axis=1).astype(BF16)
    w_ab = jnp.concatenate([w[:, o_small:o_small + 2 * n_heads],
                            jnp.zeros((d, HEAD - 2 * n_heads), w.dtype)], axis=1).astype(BF16)
    weights = (
        norm_pre[l][None, :], norm_post[l][None, :], w_main, w_ab, b_gate[l][None, :],
        conv_a_w[l], conv_qkv_w[l], _lane_row(a_log[l], n_heads), _lane_row(dt_bias[l], n_heads),
        gnorm_w[l][None, :], w_a_out[l].astype(BF16), w_b_out[l].astype(BF16), w_o[l].astype(BF16),
    )

    zeros_a = jnp.zeros((1, SUBLANES, d), F32)
    zeros_q = jnp.zeros((1, SUBLANES, 3 * d), F32)
    zeros_s = jnp.zeros((1, n_heads, HEAD, HEAD), F32)
    _, ca_m, cq_m, s_m = _seq_layer(meta[None].astype(F32), zeros_a, zeros_q, zeros_s, weights,
                                    tile=N_META, chunk=N_META, shared_state=True)
    tile = min(256, x_prompt.shape[1])
    y_p, ca_p, cq_p, s_p = _seq_layer(x_prompt, ca_m, cq_m, s_m, weights,
                                      tile=tile, chunk=CHUNK, shared_state=True)
    t_s = x_sample.shape[1]
    y_s, ca_s, cq_s, s_s = _seq_layer(x_sample, _tail_rows(state_conv_a[l], CONV_A_WIDTH),
                                      _tail_rows(state_conv_qkv[l], CONV_QKV_WIDTH), state_delta[l], weights,
                                      tile=t_s, chunk=t_s, shared_state=False)

    def tails(c, width):
        return c[None, :, SUBLANES - (width - 1):, :]

    return (y_p, y_s, tails(ca_p, CONV_A_WIDTH), tails(cq_p, CONV_QKV_WIDTH), s_p[None],
            tails(ca_s, CONV_A_WIDTH), tails(cq_s, CONV_QKV_WIDTH), s_s[None])
```

```python
import functools
import math

import jax
import jax.numpy as jnp
from jax import lax
from jax.experimental import pallas as pl
from jax.experimental.pallas import tpu as pltpu

HEAD = 128
CONV_A_WIDTH = 3
CONV_QKV_WIDTH = 4
CHUNK = 64
N_META = 16
EPS = 1e-6
SUBLANES = 8
BF16 = jnp.bfloat16
F32 = jnp.float32


def _dot(a, b):
    return jnp.dot(a.astype(BF16), b.astype(BF16), preferred_element_type=F32)


def _dot_nt(a, b):
    return lax.dot_general(a.astype(BF16), b.astype(BF16), (((1,), (1,)), ((), ())),
                           preferred_element_type=F32)


def _dot_tn(a, b):
    return lax.dot_general(a.astype(BF16), b.astype(BF16), (((0,), (0,)), ((), ())),
                           preferred_element_type=F32)


def _sigmoid(x):
    return 1.0 / (1.0 + jnp.exp(-x))


def _silu(x):
    return x * _sigmoid(x)


def _softplus(x):
    return jnp.maximum(x, 0.0) + jnp.log1p(jnp.exp(-jnp.abs(x)))


def _rms(x, w):
    return x * lax.rsqrt(jnp.mean(x * x, axis=-1, keepdims=True) + EPS) * w


def _chunk_cumsum(x, chunk):
    row = lax.broadcasted_iota(jnp.int32, x.shape, 0) & (chunk - 1)
    sh = 1
    while sh < chunk:
        x = x + jnp.where(row >= sh, pltpu.roll(x, sh, 0), 0.0)
        sh *= 2
    return x


def _block_diag(cat, blk_mask):
    n = cat.shape[1] // cat.shape[0]
    if n == 1:
        return cat
    return jnp.where(blk_mask, jnp.concatenate([cat] * n, axis=0), 0.0)


def _diag_blocks(bd, chunk):
    out = bd[0:chunk]
    for c in range(1, bd.shape[0] // chunk):
        out = out + bd[c * chunk:(c + 1) * chunk]
    return out


def _delta_tile(qkv_scr, beta_all, g_all, s_ref, o_scr, gnw, *, chunk):
    tt = qkv_scr.shape[0]
    d = qkv_scr.shape[1] // 3
    n_heads = d // HEAD
    n_c = tt // chunk
    shift = int(math.log2(chunk))
    ii = lax.broadcasted_iota(jnp.int32, (tt, tt), 0)
    jj = lax.broadcasted_iota(jnp.int32, (tt, tt), 1)
    blk = (ii >> shift) == (jj >> shift)
    incl = blk & (ii >= jj)
    strict = blk & (ii > jj)
    ci = lax.broadcasted_iota(jnp.int32, (chunk, tt), 0)
    cj = lax.broadcasted_iota(jnp.int32, (chunk, tt), 1) & (chunk - 1)
    eye_cat = jnp.where(ci == cj, 1.0, 0.0).astype(F32)

    g_cum_all = _chunk_cumsum(g_all, chunk)
    g_cum_t = g_cum_all.T
    e_g_all = jnp.exp(g_cum_all)

    qs, ks, vbs, kbes, gcs, attns, ms, ps = [], [], [], [], [], [], [], []
    for h in range(n_heads):
        q = qkv_scr[:, h * HEAD:(h + 1) * HEAD]
        k = qkv_scr[:, d + h * HEAD:d + (h + 1) * HEAD]
        v = qkv_scr[:, 2 * d + h * HEAD:2 * d + (h + 1) * HEAD]
        q = q * (lax.rsqrt(jnp.sum(q * q, axis=-1, keepdims=True) + EPS) * (HEAD ** -0.5))
        k = k * lax.rsqrt(jnp.sum(k * k, axis=-1, keepdims=True) + EPS)
        beta = beta_all[:, h:h + 1]
        g_cum = g_cum_all[:, n_heads + h:n_heads + h + 1]
        e_g = e_g_all[:, n_heads + h:n_heads + h + 1]
        diff = g_cum - g_cum_t[n_heads + h:n_heads + h + 1, :]
        decay = jnp.where(incl, jnp.exp(jnp.where(incl, diff, 0.0)), 0.0)
        kb = k * beta
        a_bd = jnp.where(strict, _dot_nt(kb, k) * decay, 0.0)
        attns.append(_dot_nt(q, k) * decay)
        m = -_diag_blocks(a_bd, chunk)
        ms.append(m)
        ps.append(eye_cat + m)
        qs.append(q * e_g)
        ks.append(k)
        vbs.append(v * beta)
        kbes.append(kb * e_g)
        gcs.append(g_cum)

    for _ in range(shift - 1):
        ms = [_dot(m, _block_diag(m, blk)) for m in ms]
        ps = [p + _dot(p, _block_diag(m, blk)) for p, m in zip(ps, ms)]
    xs = [_dot(_block_diag(p, blk), jnp.concatenate([vb, kbe], axis=1)) for p, vb, kbe in zip(ps, vbs, kbes)]

    v_news = [[] for _ in range(n_heads)]
    o_parts = [[] for _ in range(n_heads)]
    for c in range(n_c):
        rows = slice(c * chunk, (c + 1) * chunk)
        rs = [_dot(jnp.concatenate([xs[h][rows, HEAD:], qs[h][rows]], axis=0), s_ref[0, h])
              for h in range(n_heads)]
        for h in range(n_heads):
            v_new = xs[h][rows, :HEAD] - rs[h][:chunk]
            g_cum = gcs[h][rows]
            g_last = g_cum[chunk - 1:chunk, :]
            s_ref[0, h] = s_ref[0, h] * jnp.exp(g_last) + _dot_tn(ks[h][rows] * jnp.exp(g_last - g_cum), v_new)
            v_news[h].append(v_new)
            o_parts[h].append(rs[h][chunk:])
    for h in range(n_heads):
        o = jnp.concatenate(o_parts[h], axis=0) + _dot(attns[h], jnp.concatenate(v_news[h], axis=0))
        o_scr[:, h * HEAD:(h + 1) * HEAD] = _rms(o, gnw)


def _seq_kernel(x_ref, ca0_ref, cq0_ref, s0_ref, npre_ref, npost_ref, wmain_ref, wab_ref, bgate_ref,
                caw_ref, cqw_ref, alog_ref, dtb_ref, gnw_ref, wa_ref, wb_ref, wo_ref,
                y_ref, ca_ref, cq_ref, s_ref,
                ubuf, qbuf, qkv_scr, o_scr, *, chunk):
    tt, d = x_ref.shape[1], x_ref.shape[2]
    n_heads = d // HEAD
    t_idx = pl.program_id(1)

    @pl.when(t_idx == 0)
    def _():
        ubuf[0:SUBLANES, :] = ca0_ref[0]
        qbuf[0:SUBLANES, :] = cq0_ref[0]
        s_ref[0] = s0_ref[0]

    @pl.when(t_idx > 0)
    def _():
        ubuf[0:SUBLANES, :] = ubuf[tt:tt + SUBLANES, :]
        qbuf[0:SUBLANES, :] = qbuf[tt:tt + SUBLANES, :]

    x = x_ref[0]
    xn = _rms(x, npre_ref[...]).astype(BF16)

    pa = jnp.dot(xn, wmain_ref[:, 0:4 * d], preferred_element_type=F32)
    ubuf[SUBLANES:SUBLANES + tt, :] = pa[:, 2 * d:3 * d] * pa[:, 0:d]
    conv_u = ubuf[SUBLANES:SUBLANES + tt, :] * caw_ref[CONV_A_WIDTH - 1:CONV_A_WIDTH, :]
    for j in range(CONV_A_WIDTH - 1):
        off = SUBLANES - (CONV_A_WIDTH - 1) + j
        conv_u = conv_u + ubuf[off:off + tt, :] * caw_ref[j:j + 1, :]
    y_a = pa[:, d:2 * d] * conv_u * _silu(pa[:, 3 * d:4 * d])
    gate_a = _sigmoid(jnp.dot(xn, wmain_ref[:, 8 * d:9 * d], preferred_element_type=F32) + bgate_ref[:, 0:d])
    merged = gate_a * _dot(y_a, wa_ref[...])
    ca_ref[0] = ubuf[tt:tt + SUBLANES, :]

    qbuf[SUBLANES:SUBLANES + tt, :] = jnp.dot(xn, wmain_ref[:, 4 * d:7 * d], preferred_element_type=F32)
    conv_q = qbuf[SUBLANES:SUBLANES + tt, :] * cqw_ref[CONV_QKV_WIDTH - 1:CONV_QKV_WIDTH, :]
    for j in range(CONV_QKV_WIDTH - 1):
        off = SUBLANES - (CONV_QKV_WIDTH - 1) + j
        conv_q = conv_q + qbuf[off:off + tt, :] * cqw_ref[j:j + 1, :]
    qkv_scr[...] = _silu(conv_q)
    cq_ref[0] = qbuf[tt:tt + SUBLANES, :]

    ab = jnp.dot(xn, wab_ref[...], preferred_element_type=F32)
    beta_all = _sigmoid(ab)
    g_all = -jnp.exp(alog_ref[...]) * _softplus(ab + dtb_ref[...])

    _delta_tile(qkv_scr, beta_all, g_all, s_ref, o_scr, gnw_ref[...], chunk=chunk)

    z_b = jnp.dot(xn, wmain_ref[:, 7 * d:8 * d], preferred_element_type=F32)
    y_b = o_scr[...] * _silu(z_b)
    gate_b = _sigmoid(jnp.dot(xn, wmain_ref[:, 9 * d:10 * d], preferred_element_type=F32) + bgate_ref[:, d:2 * d])
    merged = merged + gate_b * _dot(y_b, wb_ref[...])
    y_ref[0] = x + _rms(_dot(merged, wo_ref[...]), npost_ref[...])


def _const_spec(shape):
    nd = len(shape)
    return pl.BlockSpec(shape, lambda b, t: (0,) * nd)


def _seq_layer(x, ca0, cq0, s0, weights, *, tile, chunk, shared_state):
    n, t_len, d = x.shape
    n_heads = d // HEAD
    state_map = (lambda b, t: (0, 0, 0)) if shared_state else (lambda b, t: (b, 0, 0))
    s_map = (lambda b, t: (0, 0, 0, 0)) if shared_state else (lambda b, t: (b, 0, 0, 0))
    in_specs = [
        pl.BlockSpec((1, tile, d), lambda b, t: (b, t, 0)),
        pl.BlockSpec((1, SUBLANES, d), state_map),
        pl.BlockSpec((1, SUBLANES, 3 * d), state_map),
        pl.BlockSpec((1, n_heads, HEAD, HEAD), s_map),
    ] + [_const_spec(w.shape) for w in weights]
    out_shape = (
        jax.ShapeDtypeStruct((n, t_len, d), F32),
        jax.ShapeDtypeStruct((n, SUBLANES, d), F32),
        jax.ShapeDtypeStruct((n, SUBLANES, 3 * d), F32),
        jax.ShapeDtypeStruct((n, n_heads, HEAD, HEAD), F32),
    )
    out_specs = (
        pl.BlockSpec((1, tile, d), lambda b, t: (b, t, 0)),
        pl.BlockSpec((1, SUBLANES, d), lambda b, t: (b, 0, 0)),
        pl.BlockSpec((1, SUBLANES, 3 * d), lambda b, t: (b, 0, 0)),
        pl.BlockSpec((1, n_heads, HEAD, HEAD), lambda b, t: (b, 0, 0, 0)),
    )
    scratch = [
        pltpu.VMEM((SUBLANES + tile, d), F32),
        pltpu.VMEM((SUBLANES + tile, 3 * d), F32),
        pltpu.VMEM((tile, 3 * d), F32),
        pltpu.VMEM((tile, d), F32),
    ]
    return pl.pallas_call(
        functools.partial(_seq_kernel, chunk=chunk),
        grid=(n, t_len // tile),
        in_specs=in_specs, out_specs=out_specs, out_shape=out_shape, scratch_shapes=scratch,
        compiler_params=pltpu.CompilerParams(dimension_semantics=("arbitrary", "arbitrary"),
                                             vmem_limit_bytes=56 * 1024 * 1024),
    )(x, ca0, cq0, s0, *weights)


SEQ_BLOCK = SUBLANES
PAIR_ROWS = 4 * SUBLANES


def _blocked_conv(prefix_ref, new, w_ref, out_ref, tail_ref, *, width, t_len, act):
    pw = (width - 1) * SEQ_BLOCK
    br = t_len * SEQ_BLOCK
    for j in range(new.shape[0] // br):
        ext = jnp.concatenate([prefix_ref[j * pw:(j + 1) * pw, :], new[j * br:(j + 1) * br, :]], axis=0)
        conv = ext[pw:pw + br] * w_ref[width - 1:width, :]
        for tap in range(width - 1):
            conv = conv + ext[tap * SEQ_BLOCK:tap * SEQ_BLOCK + br] * w_ref[tap:tap + 1, :]
        out_ref[j * br:(j + 1) * br, :] = act(conv)
        tail_ref[j * pw:(j + 1) * pw, :] = ext[br:br + pw]


def _time_slab(a, t, t_len, lanes=slice(None)):
    n_blocks = a.shape[0] // (t_len * SEQ_BLOCK)
    return jnp.concatenate([a[(j * t_len + t) * SEQ_BLOCK:(j * t_len + t + 1) * SEQ_BLOCK, lanes]
                            for j in range(n_blocks)], axis=0)


def _sample_pre_kernel(x_ref, cq0_ref, npre_ref, wq_ref, wk_ref, wv_ref, wbeta_ref, walpha_ref, cqw_ref,
                       alog_ref, dtb_ref, cq_ref, pre_ref, qkv_scr, *, t_len):
    rows, d = x_ref.shape
    n_heads = d // HEAD
    n_blocks = rows // (t_len * SEQ_BLOCK)
    n_seq = n_blocks * SEQ_BLOCK
    assert 2 * t_len == SUBLANES
    xn = _rms(x_ref[...], npre_ref[...]).astype(BF16)
    qkv_pre = jnp.concatenate([jnp.dot(xn, w[...], preferred_element_type=F32) for w in (wq_ref, wk_ref, wv_ref)],
                              axis=1)
    _blocked_conv(cq0_ref, qkv_pre, cqw_ref, qkv_scr, cq_ref, width=CONV_QKV_WIDTH, t_len=t_len, act=_silu)

    beta_full = _sigmoid(jnp.dot(xn, wbeta_ref[...], preferred_element_type=F32))
    g_full = -jnp.exp(alog_ref[...]) * _softplus(jnp.dot(xn, walpha_ref[...], preferred_element_type=F32)
                                                 + dtb_ref[...])
    beta = [_time_slab(beta_full, t, t_len) for t in range(t_len)]
    g_cum = []
    for t in range(t_len):
        g_t = _time_slab(g_full, t, t_len)
        g_cum.append(g_t if t == 0 else g_cum[-1] + g_t)
    e_g = [jnp.exp(g) for g in g_cum]
    decay = {(i, j): jnp.exp(g_cum[i] - g_cum[j]) for i in range(t_len) for j in range(i + 1)}
    k_decay = [jnp.exp(g_cum[-1] - g) for g in g_cum]
    zeros = jnp.zeros((n_seq, HEAD), F32)

    for h in range(n_heads):
        def col(a):
            return a[:, h:h + 1]
        q = [_time_slab(qkv_scr, t, t_len, slice(h * HEAD, (h + 1) * HEAD)) for t in range(t_len)]
        k = [_time_slab(qkv_scr, t, t_len, slice(d + h * HEAD, d + (h + 1) * HEAD)) for t in range(t_len)]
        v = [_time_slab(qkv_scr, t, t_len, slice(2 * d + h * HEAD, 2 * d + (h + 1) * HEAD)) for t in range(t_len)]
        q = [a * (lax.rsqrt(jnp.sum(a * a, axis=-1, keepdims=True) + EPS) * (HEAD ** -0.5)) for a in q]
        k = [a * lax.rsqrt(jnp.sum(a * a, axis=-1, keepdims=True) + EPS) for a in k]
        kb = [k[t] * col(beta[t]) for t in range(t_len)]
        a_low = {(i, j): jnp.sum(kb[i] * k[j], axis=-1, keepdims=True) * col(decay[i, j])
                 for i in range(t_len) for j in range(i)}
        attn = {(i, j): jnp.sum(q[i] * k[j], axis=-1, keepdims=True) * col(decay[i, j])
                for i in range(t_len) for j in range(i + 1)}
        t_inv = {}
        for i in range(t_len):
            for j in range(i):
                acc = a_low[i, j]
                for m in range(j + 1, i):
                    acc = acc + a_low[i, m] * t_inv[m, j]
                t_inv[i, j] = -acc
        vb = [v[t] * col(beta[t]) for t in range(t_len)]
        kbe = [kb[t] * col(e_g[t]) for t in range(t_len)]
        value, k_cum = [], []
        for i in range(t_len):
            val, kc = vb[i], kbe[i]
            for j in range(i):
                val = val + t_inv[i, j] * vb[j]
                kc = kc + t_inv[i, j] * kbe[j]
            value.append(val)
            k_cum.append(kc)
        q_eff, o_loc = [], []
        for i in range(t_len):
            qe = q[i] * col(e_g[i])
            ol = None
            for j in range(i + 1):
                qe = qe - attn[i, j] * k_cum[j]
                ol = attn[i, j] * value[j] if ol is None else ol + attn[i, j] * value[j]
            q_eff.append(qe)
            o_loc.append(ol)
        k_dec = [k[t] * col(k_decay[t]) for t in range(t_len)]
        last_decay = jnp.broadcast_to(col(e_g[-1]), (n_seq, HEAD))
        slabs = (q_eff + k_cum + o_loc + value + [zeros] * t_len + k_dec
                 + [last_decay] + [zeros] * (SUBLANES - 1))
        for group, slab in enumerate(slabs):
            for j in range(n_blocks):
                base = j * PAIR_ROWS * SEQ_BLOCK + group * SEQ_BLOCK
                pre_ref[h, base:base + SEQ_BLOCK, :] = slab[j * SEQ_BLOCK:(j + 1) * SEQ_BLOCK]


def _sample_state_kernel(pre_ref, s_ref, o_ref, s_out_ref):
    n_heads = pre_ref.shape[0]
    group = SUBLANES * SEQ_BLOCK
    row = lax.broadcasted_iota(jnp.int32, (SUBLANES, HEAD), 0)
    sign = jnp.where(row < SUBLANES // 2, 1.0, -1.0).astype(F32)
    def seq_rows(first):
        return pl.ds(first, SUBLANES, stride=SEQ_BLOCK)

    for i in range(SEQ_BLOCK):
        prods = [_dot(pre_ref.at[h][seq_rows(i), :], s_ref[i, h]) for h in range(n_heads)]
        for h in range(n_heads):
            o_vnew = pre_ref.at[h][seq_rows(group + i), :] + prods[h] * sign
            k_dec = pre_ref.at[h][seq_rows(2 * group + i), :]
            last_decay = pre_ref[h, 3 * group + i:3 * group + i + 1, :]
            o_ref.at[h][seq_rows(i), :] = o_vnew
            s_out_ref[i, h] = s_ref[i, h] * last_decay + _dot_tn(k_dec, o_vnew)


def _sample_post_kernel(x_ref, ca0_ref, o_ref, npre_ref, npost_ref, wh_ref, wb_ref, wc_ref, wz_ref, wzb_ref,
                        wga_ref, wgb_ref, bgate_ref, caw_ref, gnw_ref, wa_ref, wbo_ref, wo_ref,
                        y_ref, ca_ref, conv_scr, *, t_len):
    rows, d = x_ref.shape
    n_heads = d // HEAD
    br = t_len * SEQ_BLOCK
    n_blocks = rows // br
    x = x_ref[...]
    xn = _rms(x, npre_ref[...]).astype(BF16)

    def proj(w_ref):
        return jnp.dot(xn, w_ref[...], preferred_element_type=F32)

    _blocked_conv(ca0_ref, proj(wc_ref) * proj(wh_ref), caw_ref, conv_scr, ca_ref, width=CONV_A_WIDTH, t_len=t_len,
                  act=lambda a: a)
    y_a = proj(wb_ref) * conv_scr[...] * _silu(proj(wz_ref))
    merged = _sigmoid(proj(wga_ref) + bgate_ref[:, 0:d]) * _dot(y_a, wa_ref[...])
    group = SUBLANES * SEQ_BLOCK
    o_norm = [_rms(jnp.concatenate([o_ref[h, j * group:j * group + br, :] for j in range(n_blocks)], axis=0),
                   gnw_ref[...]) for h in range(n_heads)]
    y_b = jnp.concatenate(o_norm, axis=1) * _silu(proj(wzb_ref))
    merged = merged + _sigmoid(proj(wgb_ref) + bgate_ref[:, d:2 * d]) * _dot(y_b, wbo_ref[...])
    y_ref[...] = x + _rms(_dot(merged, wo_ref[...]), npost_ref[...])


def _to_blocks(a):
    n, t, c = a.shape
    return a.reshape(n // SEQ_BLOCK, SEQ_BLOCK, t, c).transpose(0, 2, 1, 3).reshape(n * t, c)


def _from_blocks(a, t):
    n = a.shape[0] // t
    return a.reshape(n // SEQ_BLOCK, t, SEQ_BLOCK, a.shape[-1]).transpose(0, 2, 1, 3).reshape(n, t, a.shape[-1])


def _sample_layer(x, ca0, cq0, s0, w):
    n, t_len, d = x.shape
    n_heads = d // HEAD
    n_blocks = n // SEQ_BLOCK
    tile_blocks = min(8, n_blocks)
    n_tiles = n_blocks // tile_blocks
    rows = tile_blocks * t_len * SEQ_BLOCK
    xb, cab, cqb = _to_blocks(x), _to_blocks(ca0), _to_blocks(cq0)
    wm = w["w_main"]

    def col_block(k):
        return pl.BlockSpec((d, d), lambda t: (0, k))

    def whole(a):
        nd = a.ndim
        return pl.BlockSpec(a.shape, lambda t: (0,) * nd)

    def row_tile(r, c):
        return pl.BlockSpec((r, c), lambda t: (t, 0))

    params = pltpu.CompilerParams(dimension_semantics=("arbitrary",), vmem_limit_bytes=56 * 1024 * 1024)
    pa_rows = (CONV_A_WIDTH - 1) * SEQ_BLOCK * tile_blocks
    pq_rows = (CONV_QKV_WIDTH - 1) * SEQ_BLOCK * tile_blocks
    pre_rows = PAIR_ROWS * SEQ_BLOCK

    cq_new, pre = pl.pallas_call(
        functools.partial(_sample_pre_kernel, t_len=t_len),
        grid=(n_tiles,),
        in_specs=[row_tile(rows, d), row_tile(pq_rows, 3 * d), whole(w["norm_pre"]),
                  col_block(4), col_block(5), col_block(6), whole(w["w_beta"]), whole(w["w_alpha"]),
                  whole(w["conv_qkv_w"]), whole(w["a_log0"]), whole(w["dt_bias0"])],
        out_specs=(row_tile(pq_rows, 3 * d),
                   pl.BlockSpec((n_heads, tile_blocks * pre_rows, HEAD), lambda t: (0, t, 0))),
        out_shape=(jax.ShapeDtypeStruct(cqb.shape, F32),
                   jax.ShapeDtypeStruct((n_heads, n_blocks * pre_rows, HEAD), F32)),
        scratch_shapes=[pltpu.VMEM((rows, 3 * d), F32)],
        compiler_params=params,
    )(xb, cqb, w["norm_pre"], wm, wm, wm, w["w_beta"], w["w_alpha"], w["conv_qkv_w"], w["a_log0"], w["dt_bias0"])

    o_rows = SUBLANES * SEQ_BLOCK
    o_blk, s_new = pl.pallas_call(
        _sample_state_kernel,
        grid=(n_blocks,),
        in_specs=[pl.BlockSpec((n_heads, pre_rows, HEAD), lambda j: (0, j, 0)),
                  pl.BlockSpec((SEQ_BLOCK, n_heads, HEAD, HEAD), lambda j: (j, 0, 0, 0))],
        out_specs=(pl.BlockSpec((n_heads, o_rows, HEAD), lambda j: (0, j, 0)),
                   pl.BlockSpec((SEQ_BLOCK, n_heads, HEAD, HEAD), lambda j: (j, 0, 0, 0))),
        out_shape=(jax.ShapeDtypeStruct((n_heads, n_blocks * o_rows, HEAD), F32),
                   jax.ShapeDtypeStruct(s0.shape, F32)),
        compiler_params=params,
    )(pre, s0)

    y, ca_new = pl.pallas_call(
        functools.partial(_sample_post_kernel, t_len=t_len),
        grid=(n_tiles,),
        in_specs=[row_tile(rows, d), row_tile(pa_rows, d),
                  pl.BlockSpec((n_heads, tile_blocks * o_rows, HEAD), lambda t: (0, t, 0)),
                  whole(w["norm_pre"]), whole(w["norm_post"]),
                  col_block(0), col_block(1), col_block(2), col_block(3), col_block(7), col_block(8), col_block(9),
                  whole(w["b_gate"]), whole(w["conv_a_w"]), whole(w["gnorm_w"]),
                  whole(w["w_a_out"]), whole(w["w_b_out"]), whole(w["w_o"])],
        out_specs=(row_tile(rows, d), row_tile(pa_rows, d)),
        out_shape=(jax.ShapeDtypeStruct(xb.shape, F32), jax.ShapeDtypeStruct(cab.shape, F32)),
        scratch_shapes=[pltpu.VMEM((rows, d), F32)],
        compiler_params=params,
    )(xb, cab, o_blk, w["norm_pre"], w["norm_post"], wm, wm, wm, wm, wm, wm, wm,
      w["b_gate"], w["conv_a_w"], w["gnorm_w"], w["w_a_out"], w["w_b_out"], w["w_o"])

    return (_from_blocks(y, t_len), _from_blocks(ca_new, CONV_A_WIDTH - 1),
            _from_blocks(cq_new, CONV_QKV_WIDTH - 1), s_new)


def _tail_rows(state, width):
    n, r, c = state.shape
    return jnp.concatenate([jnp.zeros((n, SUBLANES - r, c), state.dtype), state], axis=1)


def _lane_row(vec, offset):
    return jnp.zeros((1, HEAD), F32).at[0, offset:offset + vec.shape[0]].set(vec.astype(F32))


def kernel(x_prompt, x_sample, state_conv_a, state_conv_qkv, state_delta, meta, norm_pre, norm_post, w_in, b_gate,
           conv_a_w, conv_qkv_w, a_log, dt_bias, gnorm_w, w_a_out, w_b_out, w_o):
    depth = w_in.shape[0]
    assert depth == 1, "single-layer trunk"
    d = x_prompt.shape[-1]
    n_heads = d // HEAD
    bsz = x_prompt.shape[0]
    l = 0
    w = w_in[l]
    o_small = 8 * d
    w_main = jnp.concatenate([w[:, :o_small], w[:, o_small + 2 * n_heads:]], axis=1).astype(BF16)
    w_ab = jnp.concatenate([w[:, o_small:o_small + 2 * n_heads],
                            jnp.zeros((d, HEAD - 2 * n_heads), w.dtype)], axis=1).astype(BF16)
    weights = (
        norm_pre[l][None, :], norm_post[l][None, :], w_main, w_ab, b_gate[l][None, :],
        conv_a_w[l], conv_qkv_w[l], _lane_row(a_log[l], n_heads), _lane_row(dt_bias[l], n_heads),
        gnorm_w[l][None, :], w_a_out[l].astype(BF16), w_b_out[l].astype(BF16), w_o[l].astype(BF16),
    )

    zeros_a = jnp.zeros((1, SUBLANES, d), F32)
    zeros_q = jnp.zeros((1, SUBLANES, 3 * d), F32)
    zeros_s = jnp.zeros((1, n_heads, HEAD, HEAD), F32)
    _, ca_m, cq_m, s_m = _seq_layer(meta[None].astype(F32), zeros_a, zeros_q, zeros_s, weights,
                                    tile=N_META, chunk=N_META, shared_state=True)
    tile = min(256, x_prompt.shape[1])
    y_p, ca_p, cq_p, s_p = _seq_layer(x_prompt, ca_m, cq_m, s_m, weights,
                                      tile=tile, chunk=CHUNK, shared_state=True)
    def small_w(cols):
        return jnp.concatenate([cols, jnp.zeros((d, HEAD - n_heads), w.dtype)], axis=1).astype(BF16)

    sample_w = dict(
        norm_pre=weights[0], norm_post=weights[1], w_main=w_main, b_gate=weights[4],
        w_beta=small_w(w[:, o_small:o_small + n_heads]),
        w_alpha=small_w(w[:, o_small + n_heads:o_small + 2 * n_heads]),
        conv_a_w=conv_a_w[l], conv_qkv_w=conv_qkv_w[l], a_log0=_lane_row(a_log[l], 0),
        dt_bias0=_lane_row(dt_bias[l], 0), gnorm_w=weights[9], w_a_out=weights[10], w_b_out=weights[11],
        w_o=weights[12],
    )
    y_s, ca_s, cq_s, s_s = _sample_layer(x_sample, state_conv_a[l], state_conv_qkv[l], state_delta[l], sample_w)

    def tails(c, width):
        return c[None, :, SUBLANES - (width - 1):, :]

    return (y_p, y_s, tails(ca_p, CONV_A_WIDTH), tails(cq_p, CONV_QKV_WIDTH), s_p[None],
            ca_s[None], cq_s[None], s_s[None])
```

```python
import functools
import math

import jax
import jax.numpy as jnp
from jax import lax
from jax.experimental import pallas as pl
from jax.experimental.pallas import tpu as pltpu

HEAD = 128
CONV_A_WIDTH = 3
CONV_QKV_WIDTH = 4
CHUNK = 64
N_META = 16
EPS = 1e-6
SUBLANES = 8
BF16 = jnp.bfloat16
F32 = jnp.float32


def _dot(a, b):
    return jnp.dot(a.astype(BF16), b.astype(BF16), preferred_element_type=F32)


def _dot_nt(a, b):
    return lax.dot_general(a.astype(BF16), b.astype(BF16), (((1,), (1,)), ((), ())),
                           preferred_element_type=F32)


def _dot_tn(a, b):
    return lax.dot_general(a.astype(BF16), b.astype(BF16), (((0,), (0,)), ((), ())),
                           preferred_element_type=F32)


def _sigmoid(x):
    return 1.0 / (1.0 + jnp.exp(-x))


def _silu(x):
    return x * _sigmoid(x)


def _softplus(x):
    return jnp.maximum(x, 0.0) + jnp.log1p(jnp.exp(-jnp.abs(x)))


def _rms(x, w):
    return x * lax.rsqrt(jnp.mean(x * x, axis=-1, keepdims=True) + EPS) * w


def _chunk_cumsum(x, chunk):
    row = lax.broadcasted_iota(jnp.int32, x.shape, 0) & (chunk - 1)
    sh = 1
    while sh < chunk:
        x = x + jnp.where(row >= sh, pltpu.roll(x, sh, 0), 0.0)
        sh *= 2
    return x


def _block_diag(cat, blk_mask):
    n = cat.shape[1] // cat.shape[0]
    cat = cat.astype(BF16)
    if n == 1:
        return cat
    return jnp.where(blk_mask, jnp.concatenate([cat] * n, axis=0), jnp.zeros((), BF16))


def _diag_blocks(bd, chunk):
    out = bd[0:chunk]
    for c in range(1, bd.shape[0] // chunk):
        out = out + bd[c * chunk:(c + 1) * chunk]
    return out


def _delta_tile(qkv_scr, beta_all, g_all, s_ref, o_scr, gnw, *, chunk):
    tt = qkv_scr.shape[0]
    d = qkv_scr.shape[1] // 3
    n_heads = d // HEAD
    n_c = tt // chunk
    shift = int(math.log2(chunk))
    ii = lax.broadcasted_iota(jnp.int32, (tt, tt), 0)
    jj = lax.broadcasted_iota(jnp.int32, (tt, tt), 1)
    blk = (ii >> shift) == (jj >> shift)
    incl = blk & (ii >= jj)
    strict = blk & (ii > jj)
    ci = lax.broadcasted_iota(jnp.int32, (chunk, tt), 0)
    cj = lax.broadcasted_iota(jnp.int32, (chunk, tt), 1) & (chunk - 1)
    eye_cat = jnp.where(ci == cj, 1.0, 0.0).astype(F32)

    g_cum_all = _chunk_cumsum(g_all, chunk)
    g_cum_t = g_cum_all.T
    e_g_all = jnp.exp(g_cum_all)

    qs, ks, vbs, kbes, gcs, attns, ms, ps = [], [], [], [], [], [], [], []
    for h in range(n_heads):
        q = qkv_scr[:, h * HEAD:(h + 1) * HEAD]
        k = qkv_scr[:, d + h * HEAD:d + (h + 1) * HEAD]
        v = qkv_scr[:, 2 * d + h * HEAD:2 * d + (h + 1) * HEAD]
        q = q * (lax.rsqrt(jnp.sum(q * q, axis=-1, keepdims=True) + EPS) * (HEAD ** -0.5))
        k = k * lax.rsqrt(jnp.sum(k * k, axis=-1, keepdims=True) + EPS)
        beta = beta_all[:, h:h + 1]
        g_cum = g_cum_all[:, n_heads + h:n_heads + h + 1]
        e_g = e_g_all[:, n_heads + h:n_heads + h + 1]
        diff = g_cum - g_cum_t[n_heads + h:n_heads + h + 1, :]
        decay = jnp.where(incl, jnp.exp(jnp.where(incl, diff, 0.0)), 0.0)
        kb = k * beta
        a_bd = jnp.where(strict, _dot_nt(kb, k) * decay, 0.0)
        attns.append(_dot_nt(q, k) * decay)
        m = -_diag_blocks(a_bd, chunk)
        ms.append(m)
        ps.append(eye_cat + m)
        qs.append(q * e_g)
        ks.append(k)
        vbs.append(v * beta)
        kbes.append(kb * e_g)
        gcs.append(g_cum)
        if h % 2 == 1:
            yield

    for _ in range(shift - 1):
        ms = [_dot(m, _block_diag(m, blk)) for m in ms]
        ps = [p + _dot(p, _block_diag(m, blk)) for p, m in zip(ps, ms)]
        yield
    xs = [_dot(_block_diag(p, blk), jnp.concatenate([vb, kbe], axis=1)) for p, vb, kbe in zip(ps, vbs, kbes)]
    yield

    v_news = [[] for _ in range(n_heads)]
    o_parts = [[] for _ in range(n_heads)]
    for c in range(n_c):
        rows = slice(c * chunk, (c + 1) * chunk)
        rs = [_dot(jnp.concatenate([xs[h][rows, HEAD:], qs[h][rows]], axis=0), s_ref[0, h])
              for h in range(n_heads)]
        for h in range(n_heads):
            v_new = xs[h][rows, :HEAD] - rs[h][:chunk]
            g_cum = gcs[h][rows]
            g_last = g_cum[chunk - 1:chunk, :]
            s_ref[0, h] = s_ref[0, h] * jnp.exp(g_last) + _dot_tn(ks[h][rows] * jnp.exp(g_last - g_cum), v_new)
            v_news[h].append(v_new)
            o_parts[h].append(rs[h][chunk:])
        yield
    for h in range(n_heads):
        o = jnp.concatenate(o_parts[h], axis=0) + _dot(attns[h], jnp.concatenate(v_news[h], axis=0))
        o_scr[:, h * HEAD:(h + 1) * HEAD] = _rms(o, gnw)


_EXHAUSTED = object()


def _interleave(first, second, lead):
    for _ in range(lead):
        next(first, None)
    live = [first, second]
    while live:
        live = [g for g in live if next(g, _EXHAUSTED) is not _EXHAUSTED]


def _seq_kernel(x_ref, ca0_ref, cq0_ref, s0_ref, npre_ref, npost_ref, wmain_ref, wab_ref, bgate_ref,
                caw_ref, cqw_ref, alog_ref, dtb_ref, gnw_ref, wa_ref, wb_ref, wo_ref,
                y_ref, ca_ref, cq_ref, s_ref,
                ubuf, qbuf, qkv_scr, o_scr, *, chunk):
    tt, d = x_ref.shape[1], x_ref.shape[2]
    t_idx = pl.program_id(1)

    @pl.when(t_idx == 0)
    def _():
        ubuf[0:SUBLANES, :] = ca0_ref[0]
        qbuf[0:SUBLANES, :] = cq0_ref[0]
        s_ref[0] = s0_ref[0]

    @pl.when(t_idx > 0)
    def _():
        ubuf[0:SUBLANES, :] = ubuf[tt:tt + SUBLANES, :]
        qbuf[0:SUBLANES, :] = qbuf[tt:tt + SUBLANES, :]

    x = x_ref[0]
    xn = _rms(x, npre_ref[...]).astype(BF16)

    def proj(k):
        return jnp.dot(xn, wmain_ref[:, k * d:(k + 1) * d], preferred_element_type=F32)

    def branch_b():
        for k in range(3):
            qbuf[SUBLANES:SUBLANES + tt, k * d:(k + 1) * d] = proj(4 + k)
            yield
        conv_q = qbuf[SUBLANES:SUBLANES + tt, :] * cqw_ref[CONV_QKV_WIDTH - 1:CONV_QKV_WIDTH, :]
        for j in range(CONV_QKV_WIDTH - 1):
            off = SUBLANES - (CONV_QKV_WIDTH - 1) + j
            conv_q = conv_q + qbuf[off:off + tt, :] * cqw_ref[j:j + 1, :]
        qkv_scr[...] = _silu(conv_q)
        cq_ref[0] = qbuf[tt:tt + SUBLANES, :]
        ab = jnp.dot(xn, wab_ref[...], preferred_element_type=F32)
        beta_all = _sigmoid(ab)
        g_all = -jnp.exp(alog_ref[...]) * _softplus(ab + dtb_ref[...])
        yield
        yield from _delta_tile(qkv_scr, beta_all, g_all, s_ref, o_scr, gnw_ref[...], chunk=chunk)

    side = {}

    def halves(lhs, w_ref, col0):
        lo = jnp.dot(lhs, w_ref[:, col0:col0 + d // 2], preferred_element_type=F32)
        yield
        hi = jnp.dot(lhs, w_ref[:, col0 + d // 2:col0 + d], preferred_element_type=F32)
        return jnp.concatenate([lo, hi], axis=1)

    def branch_a():
        h_a = yield from halves(xn, wmain_ref, 0)
        yield
        c_a = yield from halves(xn, wmain_ref, 2 * d)
        ubuf[SUBLANES:SUBLANES + tt, :] = c_a * h_a
        yield
        conv_u = ubuf[SUBLANES:SUBLANES + tt, :] * caw_ref[CONV_A_WIDTH - 1:CONV_A_WIDTH, :]
        for j in range(CONV_A_WIDTH - 1):
            off = SUBLANES - (CONV_A_WIDTH - 1) + j
            conv_u = conv_u + ubuf[off:off + tt, :] * caw_ref[j:j + 1, :]
        ca_ref[0] = ubuf[tt:tt + SUBLANES, :]
        b_a = yield from halves(xn, wmain_ref, d)
        y_a = b_a * conv_u
        yield
        z_a = yield from halves(xn, wmain_ref, 3 * d)
        y_a = (y_a * _silu(z_a)).astype(BF16)
        yield
        gate_a = yield from halves(xn, wmain_ref, 8 * d)
        gate_a = _sigmoid(gate_a + bgate_ref[:, 0:d])
        yield
        ya_p = yield from halves(y_a, wa_ref, 0)
        side["merged"] = gate_a * ya_p
        yield
        z_b = yield from halves(xn, wmain_ref, 7 * d)
        side["silu_z_b"] = _silu(z_b)
        yield
        gate_b = yield from halves(xn, wmain_ref, 9 * d)
        side["gate_b"] = _sigmoid(gate_b + bgate_ref[:, d:2 * d])

    _interleave(branch_b(), branch_a(), lead=2)

    y_b = o_scr[...] * side["silu_z_b"]
    merged = side["merged"] + side["gate_b"] * _dot(y_b, wb_ref[...])
    y_ref[0] = x + _rms(_dot(merged, wo_ref[...]), npost_ref[...])


def _const_spec(shape):
    nd = len(shape)
    return pl.BlockSpec(shape, lambda b, t: (0,) * nd)


def _seq_layer(x, ca0, cq0, s0, weights, *, tile, chunk, shared_state):
    n, t_len, d = x.shape
    n_heads = d // HEAD
    state_map = (lambda b, t: (0, 0, 0)) if shared_state else (lambda b, t: (b, 0, 0))
    s_map = (lambda b, t: (0, 0, 0, 0)) if shared_state else (lambda b, t: (b, 0, 0, 0))
    in_specs = [
        pl.BlockSpec((1, tile, d), lambda b, t: (b, t, 0)),
        pl.BlockSpec((1, SUBLANES, d), state_map),
        pl.BlockSpec((1, SUBLANES, 3 * d), state_map),
        pl.BlockSpec((1, n_heads, HEAD, HEAD), s_map),
    ] + [_const_spec(w.shape) for w in weights]
    out_shape = (
        jax.ShapeDtypeStruct((n, t_len, d), F32),
        jax.ShapeDtypeStruct((n, SUBLANES, d), F32),
        jax.ShapeDtypeStruct((n, SUBLANES, 3 * d), F32),
        jax.ShapeDtypeStruct((n, n_heads, HEAD, HEAD), F32),
    )
    out_specs = (
        pl.BlockSpec((1, tile, d), lambda b, t: (b, t, 0)),
        pl.BlockSpec((1, SUBLANES, d), lambda b, t: (b, 0, 0)),
        pl.BlockSpec((1, SUBLANES, 3 * d), lambda b, t: (b, 0, 0)),
        pl.BlockSpec((1, n_heads, HEAD, HEAD), lambda b, t: (b, 0, 0, 0)),
    )
    scratch = [
        pltpu.VMEM((SUBLANES + tile, d), F32),
        pltpu.VMEM((SUBLANES + tile, 3 * d), F32),
        pltpu.VMEM((tile, 3 * d), F32),
        pltpu.VMEM((tile, d), F32),
    ]
    return pl.pallas_call(
        functools.partial(_seq_kernel, chunk=chunk),
        grid=(n, t_len // tile),
        in_specs=in_specs, out_specs=out_specs, out_shape=out_shape, scratch_shapes=scratch,
        compiler_params=pltpu.CompilerParams(dimension_semantics=("arbitrary", "arbitrary"),
                                             vmem_limit_bytes=56 * 1024 * 1024),
    )(x, ca0, cq0, s0, *weights)


SEQ_BLOCK = SUBLANES
PAIR_ROWS = 4 * SUBLANES


def _blocked_conv(prefix_ref, new, w_ref, out_ref, tail_ref, *, width, t_len, act):
    pw = (width - 1) * SEQ_BLOCK
    br = t_len * SEQ_BLOCK
    for j in range(new.shape[0] // br):
        ext = jnp.concatenate([prefix_ref[j * pw:(j + 1) * pw, :], new[j * br:(j + 1) * br, :]], axis=0)
        conv = ext[pw:pw + br] * w_ref[width - 1:width, :]
        for tap in range(width - 1):
            conv = conv + ext[tap * SEQ_BLOCK:tap * SEQ_BLOCK + br] * w_ref[tap:tap + 1, :]
        out_ref[j * br:(j + 1) * br, :] = act(conv)
        tail_ref[j * pw:(j + 1) * pw, :] = ext[br:br + pw]


def _time_slab(a, t, t_len, lanes=slice(None)):
    n_blocks = a.shape[0] // (t_len * SEQ_BLOCK)
    return jnp.concatenate([a[(j * t_len + t) * SEQ_BLOCK:(j * t_len + t + 1) * SEQ_BLOCK, lanes]
                            for j in range(n_blocks)], axis=0)


def _sample_pre_kernel(x_ref, cq0_ref, npre_ref, wq_ref, wk_ref, wv_ref, wbeta_ref, walpha_ref, cqw_ref,
                       alog_ref, dtb_ref, cq_ref, pre_ref, qkv_scr, *, t_len):
    rows, d = x_ref.shape
    n_heads = d // HEAD
    n_blocks = rows // (t_len * SEQ_BLOCK)
    n_seq = n_blocks * SEQ_BLOCK
    assert 2 * t_len == SUBLANES
    xn = _rms(x_ref[...], npre_ref[...]).astype(BF16)
    qkv_pre = jnp.concatenate([jnp.dot(xn, w[...], preferred_element_type=F32) for w in (wq_ref, wk_ref, wv_ref)],
                              axis=1)
    _blocked_conv(cq0_ref, qkv_pre, cqw_ref, qkv_scr, cq_ref, width=CONV_QKV_WIDTH, t_len=t_len, act=_silu)

    beta_full = _sigmoid(jnp.dot(xn, wbeta_ref[...], preferred_element_type=F32))
    g_full = -jnp.exp(alog_ref[...]) * _softplus(jnp.dot(xn, walpha_ref[...], preferred_element_type=F32)
                                                 + dtb_ref[...])
    beta = [_time_slab(beta_full, t, t_len) for t in range(t_len)]
    g_cum = []
    for t in range(t_len):
        g_t = _time_slab(g_full, t, t_len)
        g_cum.append(g_t if t == 0 else g_cum[-1] + g_t)
    e_g = [jnp.exp(g) for g in g_cum]
    decay = {(i, j): jnp.exp(g_cum[i] - g_cum[j]) for i in range(t_len) for j in range(i + 1)}
    k_decay = [jnp.exp(g_cum[-1] - g) for g in g_cum]
    zeros = jnp.zeros((n_seq, HEAD), F32)

    for h in range(n_heads):
        def col(a):
            return a[:, h:h + 1]
        q = [_time_slab(qkv_scr, t, t_len, slice(h * HEAD, (h + 1) * HEAD)) for t in range(t_len)]
        k = [_time_slab(qkv_scr, t, t_len, slice(d + h * HEAD, d + (h + 1) * HEAD)) for t in range(t_len)]
        v = [_time_slab(qkv_scr, t, t_len, slice(2 * d + h * HEAD, 2 * d + (h + 1) * HEAD)) for t in range(t_len)]
        q = [a * (lax.rsqrt(jnp.sum(a * a, axis=-1, keepdims=True) + EPS) * (HEAD ** -0.5)) for a in q]
        k = [a * lax.rsqrt(jnp.sum(a * a, axis=-1, keepdims=True) + EPS) for a in k]
        kb = [k[t] * col(beta[t]) for t in range(t_len)]
        a_low = {(i, j): jnp.sum(kb[i] * k[j], axis=-1, keepdims=True) * col(decay[i, j])
                 for i in range(t_len) for j in range(i)}
        attn = {(i, j): jnp.sum(q[i] * k[j], axis=-1, keepdims=True) * col(decay[i, j])
                for i in range(t_len) for j in range(i + 1)}
        t_inv = {}
        for i in range(t_len):
            for j in range(i):
                acc = a_low[i, j]
                for m in range(j + 1, i):
                    acc = acc + a_low[i, m] * t_inv[m, j]
                t_inv[i, j] = -acc
        vb = [v[t] * col(beta[t]) for t in range(t_len)]
        kbe = [kb[t] * col(e_g[t]) for t in range(t_len)]
        value, k_cum = [], []
        for i in range(t_len):
            val, kc = vb[i], kbe[i]
            for j in range(i):
                val = val + t_inv[i, j] * vb[j]
                kc = kc + t_inv[i, j] * kbe[j]
            value.append(val)
            k_cum.append(kc)
        q_eff, o_loc = [], []
        for i in range(t_len):
            qe = q[i] * col(e_g[i])
            ol = None
            for j in range(i + 1):
                qe = qe - attn[i, j] * k_cum[j]
                ol = attn[i, j] * value[j] if ol is None else ol + attn[i, j] * value[j]
            q_eff.append(qe)
            o_loc.append(ol)
        k_dec = [k[t] * col(k_decay[t]) for t in range(t_len)]
        last_decay = jnp.broadcast_to(col(e_g[-1]), (n_seq, HEAD))
        slabs = (q_eff + k_cum + o_loc + value + [zeros] * t_len + k_dec
                 + [last_decay] + [zeros] * (SUBLANES - 1))
        for group, slab in enumerate(slabs):
            for j in range(n_blocks):
                base = j * PAIR_ROWS * SEQ_BLOCK + group * SEQ_BLOCK
                pre_ref[h, base:base + SEQ_BLOCK, :] = slab[j * SEQ_BLOCK:(j + 1) * SEQ_BLOCK]


def _sample_state_kernel(pre_ref, s_ref, o_ref, s_out_ref):
    n_heads = pre_ref.shape[0]
    group = SUBLANES * SEQ_BLOCK
    row = lax.broadcasted_iota(jnp.int32, (SUBLANES, HEAD), 0)
    sign = jnp.where(row < SUBLANES // 2, 1.0, -1.0).astype(F32)
    def seq_rows(first):
        return pl.ds(first, SUBLANES, stride=SEQ_BLOCK)

    for i in range(SEQ_BLOCK):
        prods = [_dot(pre_ref.at[h][seq_rows(i), :], s_ref[i, h]) for h in range(n_heads)]
        for h in range(n_heads):
            o_vnew = pre_ref.at[h][seq_rows(group + i), :] + prods[h] * sign
            k_dec = pre_ref.at[h][seq_rows(2 * group + i), :]
            last_decay = pre_ref[h, 3 * group + i:3 * group + i + 1, :]
            o_ref.at[h][seq_rows(i), :] = o_vnew
            s_out_ref[i, h] = s_ref[i, h] * last_decay + _dot_tn(k_dec, o_vnew)


def _sample_post_kernel(x_ref, ca0_ref, o_ref, npre_ref, npost_ref, wh_ref, wb_ref, wc_ref, wz_ref, wzb_ref,
                        wga_ref, wgb_ref, bgate_ref, caw_ref, gnw_ref, wa_ref, wbo_ref, wo_ref,
                        y_ref, ca_ref, conv_scr, *, t_len):
    rows, d = x_ref.shape
    n_heads = d // HEAD
    br = t_len * SEQ_BLOCK
    n_blocks = rows // br
    x = x_ref[...]
    xn = _rms(x, npre_ref[...]).astype(BF16)

    def proj(w_ref):
        return jnp.dot(xn, w_ref[...], preferred_element_type=F32)

    _blocked_conv(ca0_ref, proj(wc_ref) * proj(wh_ref), caw_ref, conv_scr, ca_ref, width=CONV_A_WIDTH, t_len=t_len,
                  act=lambda a: a)
    y_a = proj(wb_ref) * conv_scr[...] * _silu(proj(wz_ref))
    merged = _sigmoid(proj(wga_ref) + bgate_ref[:, 0:d]) * _dot(y_a, wa_ref[...])
    group = SUBLANES * SEQ_BLOCK
    o_norm = [_rms(jnp.concatenate([o_ref[h, j * group:j * group + br, :] for j in range(n_blocks)], axis=0),
                   gnw_ref[...]) for h in range(n_heads)]
    y_b = jnp.concatenate(o_norm, axis=1) * _silu(proj(wzb_ref))
    merged = merged + _sigmoid(proj(wgb_ref) + bgate_ref[:, d:2 * d]) * _dot(y_b, wbo_ref[...])
    y_ref[...] = x + _rms(_dot(merged, wo_ref[...]), npost_ref[...])


def _to_blocks(a):
    n, t, c = a.shape
    return a.reshape(n // SEQ_BLOCK, SEQ_BLOCK, t, c).transpose(0, 2, 1, 3).reshape(n * t, c)


def _from_blocks(a, t):
    n = a.shape[0] // t
    return a.reshape(n // SEQ_BLOCK, t, SEQ_BLOCK, a.shape[-1]).transpose(0, 2, 1, 3).reshape(n, t, a.shape[-1])


def _sample_layer(x, ca0, cq0, s0, w):
    n, t_len, d = x.shape
    n_heads = d // HEAD
    n_blocks = n // SEQ_BLOCK
    tile_blocks = min(8, n_blocks)
    n_tiles = n_blocks // tile_blocks
    rows = tile_blocks * t_len * SEQ_BLOCK
    xb, cab, cqb = _to_blocks(x), _to_blocks(ca0), _to_blocks(cq0)
    wm = w["w_main"]

    def col_block(k):
        return pl.BlockSpec((d, d), lambda t: (0, k))

    def whole(a):
        nd = a.ndim
        return pl.BlockSpec(a.shape, lambda t: (0,) * nd)

    def row_tile(r, c):
        return pl.BlockSpec((r, c), lambda t: (t, 0))

    params = pltpu.CompilerParams(dimension_semantics=("arbitrary",), vmem_limit_bytes=56 * 1024 * 1024)
    pa_rows = (CONV_A_WIDTH - 1) * SEQ_BLOCK * tile_blocks
    pq_rows = (CONV_QKV_WIDTH - 1) * SEQ_BLOCK * tile_blocks
    pre_rows = PAIR_ROWS * SEQ_BLOCK

    cq_new, pre = pl.pallas_call(
        functools.partial(_sample_pre_kernel, t_len=t_len),
        grid=(n_tiles,),
        in_specs=[row_tile(rows, d), row_tile(pq_rows, 3 * d), whole(w["norm_pre"]),
                  col_block(4), col_block(5), col_block(6), whole(w["w_beta"]), whole(w["w_alpha"]),
                  whole(w["conv_qkv_w"]), whole(w["a_log0"]), whole(w["dt_bias0"])],
        out_specs=(row_tile(pq_rows, 3 * d),
                   pl.BlockSpec((n_heads, tile_blocks * pre_rows, HEAD), lambda t: (0, t, 0))),
        out_shape=(jax.ShapeDtypeStruct(cqb.shape, F32),
                   jax.ShapeDtypeStruct((n_heads, n_blocks * pre_rows, HEAD), F32)),
        scratch_shapes=[pltpu.VMEM((rows, 3 * d), F32)],
        compiler_params=params,
    )(xb, cqb, w["norm_pre"], wm, wm, wm, w["w_beta"], w["w_alpha"], w["conv_qkv_w"], w["a_log0"], w["dt_bias0"])

    o_rows = SUBLANES * SEQ_BLOCK
    o_blk, s_new = pl.pallas_call(
        _sample_state_kernel,
        grid=(n_blocks,),
        in_specs=[pl.BlockSpec((n_heads, pre_rows, HEAD), lambda j: (0, j, 0)),
                  pl.BlockSpec((SEQ_BLOCK, n_heads, HEAD, HEAD), lambda j: (j, 0, 0, 0))],
        out_specs=(pl.BlockSpec((n_heads, o_rows, HEAD), lambda j: (0, j, 0)),
                   pl.BlockSpec((SEQ_BLOCK, n_heads, HEAD, HEAD), lambda j: (j, 0, 0, 0))),
        out_shape=(jax.ShapeDtypeStruct((n_heads, n_blocks * o_rows, HEAD), F32),
                   jax.ShapeDtypeStruct(s0.shape, F32)),
        compiler_params=params,
    )(pre, s0)

    y, ca_new = pl.pallas_call(
        functools.partial(_sample_post_kernel, t_len=t_len),
        grid=(n_tiles,),
        in_specs=[row_tile(rows, d), row_tile(pa_rows, d),
                  pl.BlockSpec((n_heads, tile_blocks * o_rows, HEAD), lambda t: (0, t, 0)),
                  whole(w["norm_pre"]), whole(w["norm_post"]),
                  col_block(0), col_block(1), col_block(2), col_block(3), col_block(7), col_block(8), col_block(9),
                  whole(w["b_gate"]), whole(w["conv_a_w"]), whole(w["gnorm_w"]),
                  whole(w["w_a_out"]), whole(w["w_b_out"]), whole(w["w_o"])],
        out_specs=(row_tile(rows, d), row_tile(pa_rows, d)),
        out_shape=(jax.ShapeDtypeStruct(xb.shape, F32), jax.ShapeDtypeStruct(cab.shape, F32)),
        scratch_shapes=[pltpu.VMEM((rows, d), F32)],
        compiler_params=params,
    )(xb, cab, o_blk, w["norm_pre"], w["norm_post"], wm, wm, wm, wm, wm, wm, wm,
      w["b_gate"], w["conv_a_w"], w["gnorm_w"], w["w_a_out"], w["w_b_out"], w["w_o"])

    return (_from_blocks(y, t_len), _from_blocks(ca_new, CONV_A_WIDTH - 1),
            _from_blocks(cq_new, CONV_QKV_WIDTH - 1), s_new)


def _tail_rows(state, width):
    n, r, c = state.shape
    return jnp.concatenate([jnp.zeros((n, SUBLANES - r, c), state.dtype), state], axis=1)


def _lane_row(vec, offset):
    return jnp.zeros((1, HEAD), F32).at[0, offset:offset + vec.shape[0]].set(vec.astype(F32))


def kernel(x_prompt, x_sample, state_conv_a, state_conv_qkv, state_delta, meta, norm_pre, norm_post, w_in, b_gate,
           conv_a_w, conv_qkv_w, a_log, dt_bias, gnorm_w, w_a_out, w_b_out, w_o):
    depth = w_in.shape[0]
    assert depth == 1, "single-layer trunk"
    d = x_prompt.shape[-1]
    n_heads = d // HEAD
    bsz = x_prompt.shape[0]
    l = 0
    w = w_in[l]
    o_small = 8 * d
    w_main = jnp.concatenate([w[:, :o_small], w[:, o_small + 2 * n_heads:]], axis=1).astype(BF16)
    w_ab = jnp.concatenate([w[:, o_small:o_small + 2 * n_heads],
                            jnp.zeros((d, HEAD - 2 * n_heads), w.dtype)], axis=1).astype(BF16)
    weights = (
        norm_pre[l][None, :], norm_post[l][None, :], w_main, w_ab, b_gate[l][None, :],
        conv_a_w[l], conv_qkv_w[l], _lane_row(a_log[l], n_heads), _lane_row(dt_bias[l], n_heads),
        gnorm_w[l][None, :], w_a_out[l].astype(BF16), w_b_out[l].astype(BF16), w_o[l].astype(BF16),
    )

    zeros_a = jnp.zeros((1, SUBLANES, d), F32)
    zeros_q = jnp.zeros((1, SUBLANES, 3 * d), F32)
    zeros_s = jnp.zeros((1, n_heads, HEAD, HEAD), F32)
    _, ca_m, cq_m, s_m = _seq_layer(meta[None].astype(F32), zeros_a, zeros_q, zeros_s, weights,
                                    tile=N_META, chunk=N_META, shared_state=True)
    tile = min(256, x_prompt.shape[1])
    y_p, ca_p, cq_p, s_p = _seq_layer(x_prompt, ca_m, cq_m, s_m, weights,
                                      tile=tile, chunk=CHUNK, shared_state=True)
    def small_w(cols):
        return jnp.concatenate([cols, jnp.zeros((d, HEAD - n_heads), w.dtype)], axis=1).astype(BF16)

    sample_w = dict(
        norm_pre=weights[0], norm_post=weights[1], w_main=w_main, b_gate=weights[4],
        w_beta=small_w(w[:, o_small:o_small + n_heads]),
        w_alpha=small_w(w[:, o_small + n_heads:o_small + 2 * n_heads]),
        conv_a_w=conv_a_w[l], conv_qkv_w=conv_qkv_w[l], a_log0=_lane_row(a_log[l], 0),
        dt_bias0=_lane_row(dt_bias[l], 0), gnorm_w=weights[9], w_a_out=weights[10], w_b_out=weights[11],
        w_o=weights[12],
    )
    y_s, ca_s, cq_s, s_s = _sample_layer(x_sample, state_conv_a[l], state_conv_qkv[l], state_delta[l], sample_w)

    def tails(c, width):
        return c[None, :, SUBLANES - (width - 1):, :]

    return (y_p, y_s, tails(ca_p, CONV_A_WIDTH), tails(cq_p, CONV_QKV_WIDTH), s_p[None],
            ca_s[None], cq_s[None], s_s[None])
```

```python
import functools
import math

import jax
import jax.numpy as jnp
from jax import lax
from jax.experimental import pallas as pl
from jax.experimental.pallas import tpu as pltpu

HEAD = 128
CONV_A_WIDTH = 3
CONV_QKV_WIDTH = 4
CHUNK = 64
N_META = 16
EPS = 1e-6
SUBLANES = 8
BF16 = jnp.bfloat16
F32 = jnp.float32


def _dot(a, b):
    return jnp.dot(a.astype(BF16), b.astype(BF16), preferred_element_type=F32)


def _dot_nt(a, b):
    return lax.dot_general(a.astype(BF16), b.astype(BF16), (((1,), (1,)), ((), ())),
                           preferred_element_type=F32)


def _dot_tn(a, b):
    return lax.dot_general(a.astype(BF16), b.astype(BF16), (((0,), (0,)), ((), ())),
                           preferred_element_type=F32)


def _sigmoid(x):
    return 1.0 / (1.0 + jnp.exp(-x))


def _silu(x):
    return x * _sigmoid(x)


def _softplus(x):
    return jnp.maximum(x, 0.0) + jnp.log1p(jnp.exp(-jnp.abs(x)))


def _rms(x, w):
    return x * lax.rsqrt(jnp.mean(x * x, axis=-1, keepdims=True) + EPS) * w


def _chunk_cumsum(x, chunk):
    row = lax.broadcasted_iota(jnp.int32, x.shape, 0) & (chunk - 1)
    sh = 1
    while sh < chunk:
        x = x + jnp.where(row >= sh, pltpu.roll(x, sh, 0), 0.0)
        sh *= 2
    return x


def _block_diag(cat, blk_mask):
    n = cat.shape[1] // cat.shape[0]
    cat = cat.astype(BF16)
    if n == 1:
        return cat
    return jnp.where(blk_mask, jnp.concatenate([cat] * n, axis=0), jnp.zeros((), BF16))


def _diag_blocks(bd, chunk):
    out = bd[0:chunk]
    for c in range(1, bd.shape[0] // chunk):
        out = out + bd[c * chunk:(c + 1) * chunk]
    return out


def _delta_context(g_all, tt, chunk):
    shift = int(math.log2(chunk))
    ii = lax.broadcasted_iota(jnp.int32, (tt, tt), 0)
    jj = lax.broadcasted_iota(jnp.int32, (tt, tt), 1)
    blk = (ii >> shift) == (jj >> shift)
    ci = lax.broadcasted_iota(jnp.int32, (chunk, tt), 0)
    cj = lax.broadcasted_iota(jnp.int32, (chunk, tt), 1) & (chunk - 1)
    g_cum_all = _chunk_cumsum(g_all, chunk)
    return dict(blk=blk, incl=blk & (ii >= jj), strict=blk & (ii > jj),
                eye_cat=jnp.where(ci == cj, 1.0, 0.0).astype(F32),
                g_cum_all=g_cum_all, g_cum_t=g_cum_all.T, e_g_all=jnp.exp(g_cum_all))


def _delta_tile(ctx, qkv_scr, beta_all, s_ref, o_scr, gnw, heads, *, chunk):
    tt = qkv_scr.shape[0]
    d = qkv_scr.shape[1] // 3
    n_heads = d // HEAD
    n_c = tt // chunk
    shift = int(math.log2(chunk))
    blk, incl, strict, eye_cat = ctx["blk"], ctx["incl"], ctx["strict"], ctx["eye_cat"]
    g_cum_all, g_cum_t, e_g_all = ctx["g_cum_all"], ctx["g_cum_t"], ctx["e_g_all"]

    qs, ks, rhs, gcs, attns, ms, ps = [], [], [], [], [], [], []
    for i, h in enumerate(heads):
        q = qkv_scr[:, h * HEAD:(h + 1) * HEAD]
        k = qkv_scr[:, d + h * HEAD:d + (h + 1) * HEAD]
        v = qkv_scr[:, 2 * d + h * HEAD:2 * d + (h + 1) * HEAD]
        q = q * (lax.rsqrt(jnp.sum(q * q, axis=-1, keepdims=True) + EPS) * (HEAD ** -0.5))
        k = k * lax.rsqrt(jnp.sum(k * k, axis=-1, keepdims=True) + EPS)
        beta = beta_all[:, h:h + 1]
        g_cum = g_cum_all[:, n_heads + h:n_heads + h + 1]
        e_g = e_g_all[:, n_heads + h:n_heads + h + 1]
        diff = g_cum - g_cum_t[n_heads + h:n_heads + h + 1, :]
        decay = jnp.where(incl, jnp.exp(jnp.where(incl, diff, 0.0)), 0.0)
        kb = k * beta
        kq = _dot_nt(jnp.concatenate([kb, q], axis=0), k)
        a_bd = jnp.where(strict, kq[:tt] * decay, 0.0)
        attns.append((kq[tt:] * decay).astype(BF16))
        m = -_diag_blocks(a_bd, chunk)
        ms.append(m)
        ps.append(eye_cat + m)
        qs.append(q * e_g)
        ks.append(k)
        rhs.append(jnp.concatenate([v * beta, kb * e_g], axis=1).astype(BF16))
        gcs.append(g_cum)
        if i % 2 == 1:
            yield

    ms = [_dot(m, _block_diag(m, blk)) for m in ms]
    yield
    for k in range(1, shift):
        last = k == shift - 1
        for h in range(len(heads)):
            m_bd = _block_diag(ms[h], blk)
            if last:
                ps[h] = ps[h] + _dot(ps[h], m_bd)
            else:
                prod = _dot(jnp.concatenate([ps[h], ms[h]], axis=0), m_bd)
                ps[h] = ps[h] + prod[:chunk]
                ms[h] = prod[chunk:]
        yield
    vks = [_dot(_block_diag(p, blk), r) for p, r in zip(ps, rhs)]
    yield

    q_effs, o_locs, wbs, last_decays = [], [], [], []
    for h in range(len(heads)):
        av = _dot(attns[h], vks[h])
        o_locs.append(av[:, :HEAD])
        q_effs.append(qs[h] - av[:, HEAD:])
        wb_h, dl_h = [], []
        for c in range(n_c):
            rows = slice(c * chunk, (c + 1) * chunk)
            g_cum = gcs[h][rows]
            g_last = g_cum[chunk - 1:chunk, :]
            k_dec = ks[h][rows] * jnp.exp(g_last - g_cum)
            wb_h.append(_dot_tn(k_dec, jnp.concatenate([vks[h][rows, HEAD:], vks[h][rows, :HEAD]], axis=1)))
            dl_h.append(jnp.exp(g_last))
        wbs.append(wb_h)
        last_decays.append(dl_h)
        if h % 2 == 1:
            yield

    o_parts = [[] for _ in heads]
    for c in range(n_c):
        rows = slice(c * chunk, (c + 1) * chunk)
        for i, h in enumerate(heads):
            s = s_ref[0, h]
            prod = _dot(jnp.concatenate([wbs[i][c][:, :HEAD], q_effs[i][rows]], axis=0), s)
            s_ref[0, h] = s * last_decays[i][c] - prod[:HEAD] + wbs[i][c][:, HEAD:]
            o_parts[i].append(prod[HEAD:] + o_locs[i][rows])
        yield
    for i, h in enumerate(heads):
        o_scr[:, h * HEAD:(h + 1) * HEAD] = _rms(jnp.concatenate(o_parts[i], axis=0), gnw)


_EXHAUSTED = object()


def _alternate(first, second, lead):
    for _ in range(lead):
        next(first, None)
        yield
    live = [first, second]
    while live:
        still = []
        for g in live:
            if next(g, _EXHAUSTED) is not _EXHAUSTED:
                still.append(g)
                yield
        live = still


def _seq_kernel(x_ref, ca0_ref, cq0_ref, s0_ref, npre_ref, npost_ref, wmain_ref, wgate_ref, wab_ref, bgate_ref,
                caw_ref, cqw_ref, alog_ref, dtb_ref, gnw_ref, wa_ref, wb_ref, wo_ref,
                y_ref, ca_ref, cq_ref, s_ref,
                ubuf, qbuf, qkv_scr, o_scr, *, chunk):
    tt, d = x_ref.shape[1], x_ref.shape[2]
    t_idx = pl.program_id(1)

    @pl.when(t_idx == 0)
    def _():
        ubuf[0:SUBLANES, :] = ca0_ref[0]
        qbuf[0:SUBLANES, :] = cq0_ref[0]
        s_ref[0] = s0_ref[0]

    @pl.when(t_idx > 0)
    def _():
        ubuf[0:SUBLANES, :] = ubuf[tt:tt + SUBLANES, :]
        qbuf[0:SUBLANES, :] = qbuf[tt:tt + SUBLANES, :]

    x = x_ref[0]
    xn = _rms(x, npre_ref[...]).astype(BF16)

    def proj(k):
        return jnp.dot(xn, wmain_ref[:, k * d:(k + 1) * d], preferred_element_type=F32)

    def branch_b():
        for k in range(3):
            qbuf[SUBLANES:SUBLANES + tt, k * d:(k + 1) * d] = proj(4 + k)
            yield
        conv_q = qbuf[SUBLANES:SUBLANES + tt, :] * cqw_ref[CONV_QKV_WIDTH - 1:CONV_QKV_WIDTH, :]
        for j in range(CONV_QKV_WIDTH - 1):
            off = SUBLANES - (CONV_QKV_WIDTH - 1) + j
            conv_q = conv_q + qbuf[off:off + tt, :] * cqw_ref[j:j + 1, :]
        qkv_scr[...] = _silu(conv_q)
        cq_ref[0] = qbuf[tt:tt + SUBLANES, :]
        ab = jnp.dot(xn, wab_ref[...], preferred_element_type=F32)
        beta_all = _sigmoid(ab)
        g_all = -jnp.exp(alog_ref[...]) * _softplus(ab + dtb_ref[...])
        yield
        ctx = _delta_context(g_all, tt, chunk)
        n_heads = d // HEAD
        groups = [_delta_tile(ctx, qkv_scr, beta_all, s_ref, o_scr, gnw_ref[...], heads, chunk=chunk)
                  for heads in (range(0, n_heads // 2), range(n_heads // 2, n_heads))]
        yield from _alternate(groups[0], groups[1], lead=2)

    side = {}

    def halves(lhs, w_ref, col0):
        lo = jnp.dot(lhs, w_ref[:, col0:col0 + d // 2], preferred_element_type=F32)
        yield
        hi = jnp.dot(lhs, w_ref[:, col0 + d // 2:col0 + d], preferred_element_type=F32)
        return jnp.concatenate([lo, hi], axis=1)

    def branch_a():
        h_a = yield from halves(xn, wmain_ref, 0)
        yield
        c_a = yield from halves(xn, wmain_ref, 2 * d)
        ubuf[SUBLANES:SUBLANES + tt, :] = c_a * h_a
        yield
        conv_u = ubuf[SUBLANES:SUBLANES + tt, :] * caw_ref[CONV_A_WIDTH - 1:CONV_A_WIDTH, :]
        for j in range(CONV_A_WIDTH - 1):
            off = SUBLANES - (CONV_A_WIDTH - 1) + j
            conv_u = conv_u + ubuf[off:off + tt, :] * caw_ref[j:j + 1, :]
        ca_ref[0] = ubuf[tt:tt + SUBLANES, :]
        b_a = yield from halves(xn, wmain_ref, d)
        y_a = b_a * conv_u
        yield
        z_a = yield from halves(xn, wmain_ref, 3 * d)
        y_a = (y_a * _silu(z_a)).astype(BF16)
        yield
        gate_a = yield from halves(xn, wgate_ref, 0)
        gate_a = _sigmoid(gate_a + bgate_ref[:, 0:d])
        yield
        ya_p = yield from halves(y_a, wa_ref, 0)
        side["merged"] = gate_a * ya_p
        yield
        z_b = yield from halves(xn, wmain_ref, 7 * d)
        side["silu_z_b"] = _silu(z_b)
        yield
        gate_b = yield from halves(xn, wgate_ref, d)
        side["gate_b"] = _sigmoid(gate_b + bgate_ref[:, d:2 * d])

    for _ in _alternate(branch_b(), branch_a(), lead=2):
        pass

    y_b = o_scr[...] * side["silu_z_b"]
    merged = side["merged"] + side["gate_b"] * _dot(y_b, wb_ref[...])
    y_ref[0] = x + _rms(_dot(merged, wo_ref[...]), npost_ref[...])


def _const_spec(shape):
    nd = len(shape)
    return pl.BlockSpec(shape, lambda b, t: (0,) * nd)


def _seq_layer(x, ca0, cq0, s0, weights, *, tile, chunk, shared_state):
    n, t_len, d = x.shape
    n_heads = d // HEAD
    state_map = (lambda b, t: (0, 0, 0)) if shared_state else (lambda b, t: (b, 0, 0))
    s_map = (lambda b, t: (0, 0, 0, 0)) if shared_state else (lambda b, t: (b, 0, 0, 0))
    in_specs = [
        pl.BlockSpec((1, tile, d), lambda b, t: (b, t, 0)),
        pl.BlockSpec((1, SUBLANES, d), state_map),
        pl.BlockSpec((1, SUBLANES, 3 * d), state_map),
        pl.BlockSpec((1, n_heads, HEAD, HEAD), s_map),
    ] + [_const_spec(w.shape) for w in weights]
    in_specs[4 + 2] = _const_spec((d, 8 * d))
    out_shape = (
        jax.ShapeDtypeStruct((n, t_len, d), F32),
        jax.ShapeDtypeStruct((n, SUBLANES, d), F32),
        jax.ShapeDtypeStruct((n, SUBLANES, 3 * d), F32),
        jax.ShapeDtypeStruct((n, n_heads, HEAD, HEAD), F32),
    )
    out_specs = (
        pl.BlockSpec((1, tile, d), lambda b, t: (b, t, 0)),
        pl.BlockSpec((1, SUBLANES, d), lambda b, t: (b, 0, 0)),
        pl.BlockSpec((1, SUBLANES, 3 * d), lambda b, t: (b, 0, 0)),
        pl.BlockSpec((1, n_heads, HEAD, HEAD), lambda b, t: (b, 0, 0, 0)),
    )
    scratch = [
        pltpu.VMEM((SUBLANES + tile, d), F32),
        pltpu.VMEM((SUBLANES + tile, 3 * d), F32),
        pltpu.VMEM((tile, 3 * d), F32),
        pltpu.VMEM((tile, d), F32),
    ]
    return pl.pallas_call(
        functools.partial(_seq_kernel, chunk=chunk),
        grid=(n, t_len // tile),
        in_specs=in_specs, out_specs=out_specs, out_shape=out_shape, scratch_shapes=scratch,
        compiler_params=pltpu.CompilerParams(dimension_semantics=("arbitrary", "arbitrary"),
                                             vmem_limit_bytes=56 * 1024 * 1024),
    )(x, ca0, cq0, s0, *weights)


SEQ_BLOCK = SUBLANES
PAIR_ROWS = 4 * SUBLANES


def _blocked_conv(prefix_ref, new, w_ref, out_ref, tail_ref, *, width, t_len, act):
    pw = (width - 1) * SEQ_BLOCK
    br = t_len * SEQ_BLOCK
    for j in range(new.shape[0] // br):
        ext = jnp.concatenate([prefix_ref[j * pw:(j + 1) * pw, :], new[j * br:(j + 1) * br, :]], axis=0)
        conv = ext[pw:pw + br] * w_ref[width - 1:width, :]
        for tap in range(width - 1):
            conv = conv + ext[tap * SEQ_BLOCK:tap * SEQ_BLOCK + br] * w_ref[tap:tap + 1, :]
        out_ref[j * br:(j + 1) * br, :] = act(conv)
        tail_ref[j * pw:(j + 1) * pw, :] = ext[br:br + pw]


def _time_slab(a, t, t_len, lanes=slice(None)):
    n_blocks = a.shape[0] // (t_len * SEQ_BLOCK)
    return jnp.concatenate([a[(j * t_len + t) * SEQ_BLOCK:(j * t_len + t + 1) * SEQ_BLOCK, lanes]
                            for j in range(n_blocks)], axis=0)


def _sample_pre_kernel(x_ref, cq0_ref, npre_ref, wq_ref, wk_ref, wv_ref, wbeta_ref, walpha_ref, cqw_ref,
                       alog_ref, dtb_ref, cq_ref, pre_ref, qkv_scr, *, t_len):
    rows, d = x_ref.shape
    n_heads = d // HEAD
    n_blocks = rows // (t_len * SEQ_BLOCK)
    n_seq = n_blocks * SEQ_BLOCK
    assert 2 * t_len == SUBLANES
    xn = _rms(x_ref[...], npre_ref[...]).astype(BF16)
    qkv_pre = jnp.concatenate([jnp.dot(xn, w[...], preferred_element_type=F32) for w in (wq_ref, wk_ref, wv_ref)],
                              axis=1)
    _blocked_conv(cq0_ref, qkv_pre, cqw_ref, qkv_scr, cq_ref, width=CONV_QKV_WIDTH, t_len=t_len, act=_silu)

    beta_full = _sigmoid(jnp.dot(xn, wbeta_ref[...], preferred_element_type=F32))
    g_full = -jnp.exp(alog_ref[...]) * _softplus(jnp.dot(xn, walpha_ref[...], preferred_element_type=F32)
                                                 + dtb_ref[...])
    beta = [_time_slab(beta_full, t, t_len) for t in range(t_len)]
    g_cum = []
    for t in range(t_len):
        g_t = _time_slab(g_full, t, t_len)
        g_cum.append(g_t if t == 0 else g_cum[-1] + g_t)
    e_g = [jnp.exp(g) for g in g_cum]
    decay = {(i, j): jnp.exp(g_cum[i] - g_cum[j]) for i in range(t_len) for j in range(i + 1)}
    k_decay = [jnp.exp(g_cum[-1] - g) for g in g_cum]
    zeros = jnp.zeros((n_seq, HEAD), F32)

    for h in range(n_heads):
        def col(a):
            return a[:, h:h + 1]
        q = [_time_slab(qkv_scr, t, t_len, slice(h * HEAD, (h + 1) * HEAD)) for t in range(t_len)]
        k = [_time_slab(qkv_scr, t, t_len, slice(d + h * HEAD, d + (h + 1) * HEAD)) for t in range(t_len)]
        v = [_time_slab(qkv_scr, t, t_len, slice(2 * d + h * HEAD, 2 * d + (h + 1) * HEAD)) for t in range(t_len)]
        q = [a * (lax.rsqrt(jnp.sum(a * a, axis=-1, keepdims=True) + EPS) * (HEAD ** -0.5)) for a in q]
        k = [a * lax.rsqrt(jnp.sum(a * a, axis=-1, keepdims=True) + EPS) for a in k]
        kb = [k[t] * col(beta[t]) for t in range(t_len)]
        a_low = {(i, j): jnp.sum(kb[i] * k[j], axis=-1, keepdims=True) * col(decay[i, j])
                 for i in range(t_len) for j in range(i)}
        attn = {(i, j): jnp.sum(q[i] * k[j], axis=-1, keepdims=True) * col(decay[i, j])
                for i in range(t_len) for j in range(i + 1)}
        t_inv = {}
        for i in range(t_len):
            for j in range(i):
                acc = a_low[i, j]
                for m in range(j + 1, i):
                    acc = acc + a_low[i, m] * t_inv[m, j]
                t_inv[i, j] = -acc
        vb = [v[t] * col(beta[t]) for t in range(t_len)]
        kbe = [kb[t] * col(e_g[t]) for t in range(t_len)]
        value, k_cum = [], []
        for i in range(t_len):
            val, kc = vb[i], kbe[i]
            for j in range(i):
                val = val + t_inv[i, j] * vb[j]
                kc = kc + t_inv[i, j] * kbe[j]
            value.append(val)
            k_cum.append(kc)
        q_eff, o_loc = [], []
        for i in range(t_len):
            qe = q[i] * col(e_g[i])
            ol = None
            for j in range(i + 1):
                qe = qe - attn[i, j] * k_cum[j]
                ol = attn[i, j] * value[j] if ol is None else ol + attn[i, j] * value[j]
            q_eff.append(qe)
            o_loc.append(ol)
        k_dec = [k[t] * col(k_decay[t]) for t in range(t_len)]
        last_decay = jnp.broadcast_to(col(e_g[-1]), (n_seq, HEAD))
        slabs = (q_eff + k_cum + o_loc + value + [zeros] * t_len + k_dec
                 + [last_decay] + [zeros] * (SUBLANES - 1))
        for group, slab in enumerate(slabs):
            for j in range(n_blocks):
                base = j * PAIR_ROWS * SEQ_BLOCK + group * SEQ_BLOCK
                pre_ref[h, base:base + SEQ_BLOCK, :] = slab[j * SEQ_BLOCK:(j + 1) * SEQ_BLOCK]


def _sample_state_kernel(pre_ref, s_ref, o_ref, s_out_ref):
    n_heads = pre_ref.shape[0]
    group = SUBLANES * SEQ_BLOCK
    row = lax.broadcasted_iota(jnp.int32, (SUBLANES, HEAD), 0)
    sign = jnp.where(row < SUBLANES // 2, 1.0, -1.0).astype(F32)
    def seq_rows(first):
        return pl.ds(first, SUBLANES, stride=SEQ_BLOCK)

    for i in range(SEQ_BLOCK):
        prods = [_dot(pre_ref.at[h][seq_rows(i), :], s_ref[i, h]) for h in range(n_heads)]
        for h in range(n_heads):
            o_vnew = pre_ref.at[h][seq_rows(group + i), :] + prods[h] * sign
            k_dec = pre_ref.at[h][seq_rows(2 * group + i), :]
            last_decay = pre_ref[h, 3 * group + i:3 * group + i + 1, :]
            o_ref.at[h][seq_rows(i), :] = o_vnew
            s_out_ref[i, h] = s_ref[i, h] * last_decay + _dot_tn(k_dec, o_vnew)


def _sample_post_kernel(x_ref, ca0_ref, o_ref, npre_ref, npost_ref, wh_ref, wb_ref, wc_ref, wz_ref, wzb_ref,
                        wga_ref, wgb_ref, bgate_ref, caw_ref, gnw_ref, wa_ref, wbo_ref, wo_ref,
                        y_ref, ca_ref, conv_scr, *, t_len):
    rows, d = x_ref.shape
    n_heads = d // HEAD
    br = t_len * SEQ_BLOCK
    n_blocks = rows // br
    x = x_ref[...]
    xn = _rms(x, npre_ref[...]).astype(BF16)

    def proj(w_ref):
        return jnp.dot(xn, w_ref[...], preferred_element_type=F32)

    _blocked_conv(ca0_ref, proj(wc_ref) * proj(wh_ref), caw_ref, conv_scr, ca_ref, width=CONV_A_WIDTH, t_len=t_len,
                  act=lambda a: a)
    y_a = proj(wb_ref) * conv_scr[...] * _silu(proj(wz_ref))
    merged = _sigmoid(proj(wga_ref) + bgate_ref[:, 0:d]) * _dot(y_a, wa_ref[...])
    group = SUBLANES * SEQ_BLOCK
    o_norm = [_rms(jnp.concatenate([o_ref[h, j * group:j * group + br, :] for j in range(n_blocks)], axis=0),
                   gnw_ref[...]) for h in range(n_heads)]
    y_b = jnp.concatenate(o_norm, axis=1) * _silu(proj(wzb_ref))
    merged = merged + _sigmoid(proj(wgb_ref) + bgate_ref[:, d:2 * d]) * _dot(y_b, wbo_ref[...])
    y_ref[...] = x + _rms(_dot(merged, wo_ref[...]), npost_ref[...])


def _to_blocks(a):
    n, t, c = a.shape
    return a.reshape(n // SEQ_BLOCK, SEQ_BLOCK, t, c).transpose(0, 2, 1, 3).reshape(n * t, c)


def _from_blocks(a, t):
    n = a.shape[0] // t
    return a.reshape(n // SEQ_BLOCK, t, SEQ_BLOCK, a.shape[-1]).transpose(0, 2, 1, 3).reshape(n, t, a.shape[-1])


def _sample_layer(x, ca0, cq0, s0, w):
    n, t_len, d = x.shape
    n_heads = d // HEAD
    n_blocks = n // SEQ_BLOCK
    tile_blocks = min(8, n_blocks)
    n_tiles = n_blocks // tile_blocks
    rows = tile_blocks * t_len * SEQ_BLOCK
    xb, cab, cqb = _to_blocks(x), _to_blocks(ca0), _to_blocks(cq0)
    wm = w["w_main"]

    def col_block(k):
        return pl.BlockSpec((d, d), lambda t: (0, k))

    def whole(a):
        nd = a.ndim
        return pl.BlockSpec(a.shape, lambda t: (0,) * nd)

    def row_tile(r, c):
        return pl.BlockSpec((r, c), lambda t: (t, 0))

    params = pltpu.CompilerParams(dimension_semantics=("arbitrary",), vmem_limit_bytes=56 * 1024 * 1024)
    pa_rows = (CONV_A_WIDTH - 1) * SEQ_BLOCK * tile_blocks
    pq_rows = (CONV_QKV_WIDTH - 1) * SEQ_BLOCK * tile_blocks
    pre_rows = PAIR_ROWS * SEQ_BLOCK

    cq_new, pre = pl.pallas_call(
        functools.partial(_sample_pre_kernel, t_len=t_len),
        grid=(n_tiles,),
        in_specs=[row_tile(rows, d), row_tile(pq_rows, 3 * d), whole(w["norm_pre"]),
                  col_block(4), col_block(5), col_block(6), whole(w["w_beta"]), whole(w["w_alpha"]),
                  whole(w["conv_qkv_w"]), whole(w["a_log0"]), whole(w["dt_bias0"])],
        out_specs=(row_tile(pq_rows, 3 * d),
                   pl.BlockSpec((n_heads, tile_blocks * pre_rows, HEAD), lambda t: (0, t, 0))),
        out_shape=(jax.ShapeDtypeStruct(cqb.shape, F32),
                   jax.ShapeDtypeStruct((n_heads, n_blocks * pre_rows, HEAD), F32)),
        scratch_shapes=[pltpu.VMEM((rows, 3 * d), F32)],
        compiler_params=params,
    )(xb, cqb, w["norm_pre"], wm, wm, wm, w["w_beta"], w["w_alpha"], w["conv_qkv_w"], w["a_log0"], w["dt_bias0"])

    o_rows = SUBLANES * SEQ_BLOCK
    o_blk, s_new = pl.pallas_call(
        _sample_state_kernel,
        grid=(n_blocks,),
        in_specs=[pl.BlockSpec((n_heads, pre_rows, HEAD), lambda j: (0, j, 0)),
                  pl.BlockSpec((SEQ_BLOCK, n_heads, HEAD, HEAD), lambda j: (j, 0, 0, 0))],
        out_specs=(pl.BlockSpec((n_heads, o_rows, HEAD), lambda j: (0, j, 0)),
                   pl.BlockSpec((SEQ_BLOCK, n_heads, HEAD, HEAD), lambda j: (j, 0, 0, 0))),
        out_shape=(jax.ShapeDtypeStruct((n_heads, n_blocks * o_rows, HEAD), F32),
                   jax.ShapeDtypeStruct(s0.shape, F32)),
        compiler_params=params,
    )(pre, s0)

    y, ca_new = pl.pallas_call(
        functools.partial(_sample_post_kernel, t_len=t_len),
        grid=(n_tiles,),
        in_specs=[row_tile(rows, d), row_tile(pa_rows, d),
                  pl.BlockSpec((n_heads, tile_blocks * o_rows, HEAD), lambda t: (0, t, 0)),
                  whole(w["norm_pre"]), whole(w["norm_post"]),
                  col_block(0), col_block(1), col_block(2), col_block(3), col_block(7), col_block(0), col_block(1),
                  whole(w["b_gate"]), whole(w["conv_a_w"]), whole(w["gnorm_w"]),
                  whole(w["w_a_out"]), whole(w["w_b_out"]), whole(w["w_o"])],
        out_specs=(row_tile(rows, d), row_tile(pa_rows, d)),
        out_shape=(jax.ShapeDtypeStruct(xb.shape, F32), jax.ShapeDtypeStruct(cab.shape, F32)),
        scratch_shapes=[pltpu.VMEM((rows, d), F32)],
        compiler_params=params,
    )(xb, cab, o_blk, w["norm_pre"], w["norm_post"], wm, wm, wm, wm, wm, w["w_gates"], w["w_gates"],
      w["b_gate"], w["conv_a_w"], w["gnorm_w"], w["w_a_out"], w["w_b_out"], w["w_o"])

    return (_from_blocks(y, t_len), _from_blocks(ca_new, CONV_A_WIDTH - 1),
            _from_blocks(cq_new, CONV_QKV_WIDTH - 1), s_new)


def _tail_rows(state, width):
    n, r, c = state.shape
    return jnp.concatenate([jnp.zeros((n, SUBLANES - r, c), state.dtype), state], axis=1)


def _lane_row(vec, offset):
    return jnp.zeros((1, HEAD), F32).at[0, offset:offset + vec.shape[0]].set(vec.astype(F32))


def kernel(x_prompt, x_sample, state_conv_a, state_conv_qkv, state_delta, meta, norm_pre, norm_post, w_in, b_gate,
           conv_a_w, conv_qkv_w, a_log, dt_bias, gnorm_w, w_a_out, w_b_out, w_o):
    depth = w_in.shape[0]
    assert depth == 1, "single-layer trunk"
    d = x_prompt.shape[-1]
    n_heads = d // HEAD
    bsz = x_prompt.shape[0]
    l = 0
    w = w_in[l]
    o_small = 8 * d
    w16 = w.astype(BF16)
    w_gates = w16[:, o_small + 2 * n_heads:]
    w_ab = jnp.concatenate([w16[:, o_small:o_small + 2 * n_heads],
                            jnp.zeros((d, HEAD - 2 * n_heads), BF16)], axis=1)
    weights = (
        norm_pre[l][None, :], norm_post[l][None, :], w16, w_gates, w_ab, b_gate[l][None, :],
        conv_a_w[l], conv_qkv_w[l], _lane_row(a_log[l], n_heads), _lane_row(dt_bias[l], n_heads),
        gnorm_w[l][None, :], w_a_out[l].astype(BF16), w_b_out[l].astype(BF16), w_o[l].astype(BF16),
    )

    zeros_a = jnp.zeros((1, SUBLANES, d), F32)
    zeros_q = jnp.zeros((1, SUBLANES, 3 * d), F32)
    zeros_s = jnp.zeros((1, n_heads, HEAD, HEAD), F32)
    _, ca_m, cq_m, s_m = _seq_layer(meta[None].astype(F32), zeros_a, zeros_q, zeros_s, weights,
                                    tile=N_META, chunk=N_META, shared_state=True)
    tile = min(256, x_prompt.shape[1])
    y_p, ca_p, cq_p, s_p = _seq_layer(x_prompt, ca_m, cq_m, s_m, weights,
                                      tile=tile, chunk=CHUNK, shared_state=True)
    def small_w(cols):
        return jnp.concatenate([cols, jnp.zeros((d, HEAD - n_heads), BF16)], axis=1)

    sample_w = dict(
        norm_pre=weights[0], norm_post=weights[1], w_main=w16, w_gates=w_gates, b_gate=weights[5],
        w_beta=small_w(w16[:, o_small:o_small + n_heads]),
        w_alpha=small_w(w16[:, o_small + n_heads:o_small + 2 * n_heads]),
        conv_a_w=conv_a_w[l], conv_qkv_w=conv_qkv_w[l], a_log0=_lane_row(a_log[l], 0),
        dt_bias0=_lane_row(dt_bias[l], 0), gnorm_w=weights[10], w_a_out=weights[11], w_b_out=weights[12],
        w_o=weights[13],
    )
    y_s, ca_s, cq_s, s_s = _sample_layer(x_sample, state_conv_a[l], state_conv_qkv[l], state_delta[l], sample_w)

    def tails(c, width):
        return c[None, :, SUBLANES - (width - 1):, :]

    return (y_p, y_s, tails(ca_p, CONV_A_WIDTH), tails(cq_p, CONV_QKV_WIDTH), s_p[None],
            ca_s[None], cq_s[None], s_s[None])
```

```python
import functools
import math

import jax
import jax.numpy as jnp
from jax import lax
from jax.experimental import pallas as pl
from jax.experimental.pallas import tpu as pltpu

HEAD = 128
CONV_A_WIDTH = 3
CONV_QKV_WIDTH = 4
CHUNK = 64
N_META = 16
EPS = 1e-6
SUBLANES = 8
BF16 = jnp.bfloat16
F32 = jnp.float32


def _dot(a, b):
    return jnp.dot(a.astype(BF16), b.astype(BF16), preferred_element_type=F32)


def _dot_nt(a, b):
    return lax.dot_general(a.astype(BF16), b.astype(BF16), (((1,), (1,)), ((), ())),
                           preferred_element_type=F32)


def _dot_tn(a, b):
    return lax.dot_general(a.astype(BF16), b.astype(BF16), (((0,), (0,)), ((), ())),
                           preferred_element_type=F32)


def _sigmoid(x):
    return 1.0 / (1.0 + jnp.exp(-x))


def _silu(x):
    return x * _sigmoid(x)


def _softplus(x):
    return jnp.maximum(x, 0.0) + jnp.log1p(jnp.exp(-jnp.abs(x)))


def _rms(x, w):
    return x * lax.rsqrt(jnp.mean(x * x, axis=-1, keepdims=True) + EPS) * w


def _chunk_cumsum(x, chunk):
    row = lax.broadcasted_iota(jnp.int32, x.shape, 0) & (chunk - 1)
    sh = 1
    while sh < chunk:
        x = x + jnp.where(row >= sh, pltpu.roll(x, sh, 0), 0.0)
        sh *= 2
    return x


def _pad_lanes(piece, before, after):
    parts = [piece]
    if before:
        parts.insert(0, jnp.zeros((piece.shape[0], before), piece.dtype))
    if after:
        parts.append(jnp.zeros((piece.shape[0], after), piece.dtype))
    return jnp.concatenate(parts, axis=1) if len(parts) > 1 else piece


def _block_diag(cat):
    chunk, tt = cat.shape
    n = tt // chunk
    cat = cat.astype(BF16)
    if n == 1:
        return cat
    width = min(tt, HEAD)
    per_width = width // chunk
    lane_piece = lax.broadcasted_iota(jnp.int32, (chunk, width), 1) >> int(math.log2(chunk))
    rows = []
    for c in range(n):
        v, p = divmod(c, per_width)
        tile = cat[:, v * width:(v + 1) * width]
        if per_width > 1:
            tile = jnp.where(lane_piece == p, tile, jnp.zeros((), BF16))
        rows.append(_pad_lanes(tile, v * width, tt - (v + 1) * width))
    return jnp.concatenate(rows, axis=0)


def _delta_context(g_all, tt, chunk):
    ci = lax.broadcasted_iota(jnp.int32, (chunk, tt), 0)
    cj = lax.broadcasted_iota(jnp.int32, (chunk, tt), 1) & (chunk - 1)
    g_cum_all = _chunk_cumsum(g_all, chunk)
    return dict(incl_cat=ci >= cj, strict_cat=ci > cj,
                eye_cat=jnp.where(ci == cj, 1.0, 0.0).astype(F32),
                g_cum_all=g_cum_all, g_cum_t=g_cum_all.T, e_g_all=jnp.exp(g_cum_all))


def _column_per_chunk(col, chunk):
    tt = col.shape[0]
    width = min(tt, HEAD)
    per_width = width // chunk
    full = jnp.broadcast_to(col, (tt, width))
    lane_piece = lax.broadcasted_iota(jnp.int32, (chunk, width), 1) >> int(math.log2(chunk))
    parts = []
    for v in range(tt // width):
        acc = full[v * per_width * chunk:(v * per_width + 1) * chunk]
        for p in range(1, per_width):
            c = v * per_width + p
            acc = jnp.where(lane_piece == p, full[c * chunk:(c + 1) * chunk], acc)
        parts.append(acc)
    return jnp.concatenate(parts, axis=1)


def _chunks_side_by_side(a, chunk):
    return jnp.concatenate([a[c * chunk:(c + 1) * chunk] for c in range(a.shape[0] // chunk)], axis=1)


def _delta_tile(ctx, qkv_scr, beta_all, s_ref, o_scr, gnw, heads, *, chunk):
    tt = qkv_scr.shape[0]
    d = qkv_scr.shape[1] // 3
    n_heads = d // HEAD
    n_c = tt // chunk
    shift = int(math.log2(chunk))
    eye_cat = ctx["eye_cat"]
    incl, strict = ctx["incl_cat"], ctx["strict_cat"]
    g_cum_all, g_cum_t, e_g_all = ctx["g_cum_all"], ctx["g_cum_t"], ctx["e_g_all"]

    qs, ks, rhs, gcs, attns, ms, ps = [], [], [], [], [], [], []
    for i, h in enumerate(heads):
        q = qkv_scr[:, h * HEAD:(h + 1) * HEAD]
        k = qkv_scr[:, d + h * HEAD:d + (h + 1) * HEAD]
        v = qkv_scr[:, 2 * d + h * HEAD:2 * d + (h + 1) * HEAD]
        q = q * (lax.rsqrt(jnp.sum(q * q, axis=-1, keepdims=True) + EPS) * (HEAD ** -0.5))
        k = k * lax.rsqrt(jnp.sum(k * k, axis=-1, keepdims=True) + EPS)
        beta = beta_all[:, h:h + 1]
        g_cum = g_cum_all[:, n_heads + h:n_heads + h + 1]
        e_g = e_g_all[:, n_heads + h:n_heads + h + 1]
        diff = _column_per_chunk(g_cum, chunk) - g_cum_t[n_heads + h:n_heads + h + 1, :]
        decay = jnp.where(incl, jnp.exp(jnp.where(incl, diff, 0.0)), 0.0)
        kb = k * beta
        k16 = k.astype(BF16)
        k_bd = jnp.concatenate([_pad_lanes(k16[c * chunk:(c + 1) * chunk], c * HEAD, (n_c - 1 - c) * HEAD)
                                for c in range(n_c)], axis=0)
        kq = _dot_nt(jnp.concatenate([_chunks_side_by_side(kb, chunk), _chunks_side_by_side(q, chunk)], axis=0),
                     k_bd)
        m = -jnp.where(strict, kq[:chunk] * decay, 0.0)
        attns.append(_block_diag(kq[chunk:] * decay))
        ms.append(m)
        ps.append(eye_cat + m)
        qs.append(q * e_g)
        ks.append(k)
        rhs.append(jnp.concatenate([v * beta, kb * e_g], axis=1).astype(BF16))
        gcs.append(g_cum)
        if i % 2 == 1:
            yield

    ms = [_dot(m, _block_diag(m)) for m in ms]
    yield
    for k in range(1, shift):
        last = k == shift - 1
        for h in range(len(heads)):
            m_bd = _block_diag(ms[h])
            if last:
                ps[h] = ps[h] + _dot(ps[h], m_bd)
            else:
                prod = _dot(jnp.concatenate([ps[h], ms[h]], axis=0), m_bd)
                ps[h] = ps[h] + prod[:chunk]
                ms[h] = prod[chunk:]
        yield
    vks = [_dot(_block_diag(p), r) for p, r in zip(ps, rhs)]
    yield

    q_effs, o_locs, wbs, last_decays = [], [], [], []
    for h in range(len(heads)):
        av = _dot(attns[h], vks[h])
        o_locs.append(av[:, :HEAD])
        q_effs.append(qs[h] - av[:, HEAD:])
        wb_h, dl_h = [], []
        for c in range(n_c):
            rows = slice(c * chunk, (c + 1) * chunk)
            g_cum = gcs[h][rows]
            g_last = g_cum[chunk - 1:chunk, :]
            k_dec = ks[h][rows] * jnp.exp(g_last - g_cum)
            wb_h.append(_dot_tn(k_dec, jnp.concatenate([vks[h][rows, HEAD:], vks[h][rows, :HEAD]], axis=1)))
            dl_h.append(jnp.exp(g_last))
        wbs.append(wb_h)
        last_decays.append(dl_h)
        if h % 2 == 1:
            yield

    o_parts = [[] for _ in heads]
    for c in range(n_c):
        rows = slice(c * chunk, (c + 1) * chunk)
        for i, h in enumerate(heads):
            s = s_ref[0, h]
            prod = _dot(jnp.concatenate([wbs[i][c][:, :HEAD], q_effs[i][rows]], axis=0), s)
            s_ref[0, h] = s * last_decays[i][c] - prod[:HEAD] + wbs[i][c][:, HEAD:]
            o_parts[i].append(prod[HEAD:] + o_locs[i][rows])
        yield
    for i, h in enumerate(heads):
        o_scr[:, h * HEAD:(h + 1) * HEAD] = _rms(jnp.concatenate(o_parts[i], axis=0), gnw)


def _causal_conv(buf, w_ref, tt):
    width = w_ref.shape[0]
    out = buf[SUBLANES:SUBLANES + tt, :] * w_ref[width - 1:width, :]
    for j in range(width - 1):
        off = SUBLANES - (width - 1) + j
        out = out + buf[off:off + tt, :] * w_ref[j:j + 1, :]
    return out


_EXHAUSTED = object()


def _alternate(first, second, lead):
    for _ in range(lead):
        next(first, None)
        yield
    live = [first, second]
    while live:
        still = []
        for g in live:
            if next(g, _EXHAUSTED) is not _EXHAUSTED:
                still.append(g)
                yield
        live = still


def _seq_kernel(x_ref, ca0_ref, cq0_ref, s0_ref, npre_ref, npost_ref, wmain_ref, wgate_ref, wab_ref, bgate_ref,
                caw_ref, cqw_ref, alog_ref, dtb_ref, gnw_ref, wa_ref, wb_ref, wo_ref,
                y_ref, ca_ref, cq_ref, s_ref,
                ubuf, qbuf, qkv_scr, o_scr, *, chunk):
    tt, d = x_ref.shape[1], x_ref.shape[2]
    t_idx = pl.program_id(1)

    @pl.when(t_idx == 0)
    def _():
        ubuf[0:SUBLANES, :] = ca0_ref[0]
        qbuf[0:SUBLANES, :] = cq0_ref[0]
        s_ref[0] = s0_ref[0]

    @pl.when(t_idx > 0)
    def _():
        ubuf[0:SUBLANES, :] = ubuf[tt:tt + SUBLANES, :]
        qbuf[0:SUBLANES, :] = qbuf[tt:tt + SUBLANES, :]

    x = x_ref[0]
    xn = _rms(x, npre_ref[...]).astype(BF16)

    def proj(k):
        return jnp.dot(xn, wmain_ref[:, k * d:(k + 1) * d], preferred_element_type=F32)

    def branch_b():
        for k in range(3):
            qbuf[SUBLANES:SUBLANES + tt, k * d:(k + 1) * d] = proj(4 + k)
            yield
        qkv_scr[...] = _silu(_causal_conv(qbuf, cqw_ref, tt))
        cq_ref[0] = qbuf[tt:tt + SUBLANES, :]
        ab = jnp.dot(xn, wab_ref[...], preferred_element_type=F32)
        beta_all = _sigmoid(ab)
        g_all = -jnp.exp(alog_ref[...]) * _softplus(ab + dtb_ref[...])
        yield
        ctx = _delta_context(g_all, tt, chunk)
        n_heads = d // HEAD
        groups = [_delta_tile(ctx, qkv_scr, beta_all, s_ref, o_scr, gnw_ref[...], heads, chunk=chunk)
                  for heads in (range(0, n_heads // 2), range(n_heads // 2, n_heads))]
        yield from _alternate(groups[0], groups[1], lead=2)

    side = {}

    def halves(lhs, w_ref, col0):
        lo = jnp.dot(lhs, w_ref[:, col0:col0 + d // 2], preferred_element_type=F32)
        yield
        hi = jnp.dot(lhs, w_ref[:, col0 + d // 2:col0 + d], preferred_element_type=F32)
        return jnp.concatenate([lo, hi], axis=1)

    def branch_a():
        h_a = yield from halves(xn, wmain_ref, 0)
        yield
        c_a = yield from halves(xn, wmain_ref, 2 * d)
        ubuf[SUBLANES:SUBLANES + tt, :] = c_a * h_a
        yield
        conv_u = _causal_conv(ubuf, caw_ref, tt)
        ca_ref[0] = ubuf[tt:tt + SUBLANES, :]
        b_a = yield from halves(xn, wmain_ref, d)
        y_a = b_a * conv_u
        yield
        z_a = yield from halves(xn, wmain_ref, 3 * d)
        y_a = (y_a * _silu(z_a)).astype(BF16)
        yield
        gate_a = yield from halves(xn, wgate_ref, 0)
        gate_a = _sigmoid(gate_a + bgate_ref[:, 0:d])
        yield
        ya_p = yield from halves(y_a, wa_ref, 0)
        side["merged"] = gate_a * ya_p
        yield
        z_b = yield from halves(xn, wmain_ref, 7 * d)
        side["silu_z_b"] = _silu(z_b)
        yield
        gate_b = yield from halves(xn, wgate_ref, d)
        side["gate_b"] = _sigmoid(gate_b + bgate_ref[:, d:2 * d])

    for _ in _alternate(branch_b(), branch_a(), lead=2):
        pass

    y_b = o_scr[...] * side["silu_z_b"]
    merged = side["merged"] + side["gate_b"] * _dot(y_b, wb_ref[...])
    y_ref[0] = x + _rms(_dot(merged, wo_ref[...]), npost_ref[...])


def _const_spec(shape):
    nd = len(shape)
    return pl.BlockSpec(shape, lambda b, t: (0,) * nd)


def _seq_layer(x, ca0, cq0, s0, weights, *, tile, chunk, shared_state):
    n, t_len, d = x.shape
    n_heads = d // HEAD
    state_map = (lambda b, t: (0, 0, 0)) if shared_state else (lambda b, t: (b, 0, 0))
    s_map = (lambda b, t: (0, 0, 0, 0)) if shared_state else (lambda b, t: (b, 0, 0, 0))
    in_specs = [
        pl.BlockSpec((1, tile, d), lambda b, t: (b, t, 0)),
        pl.BlockSpec((1, SUBLANES, d), state_map),
        pl.BlockSpec((1, SUBLANES, 3 * d), state_map),
        pl.BlockSpec((1, n_heads, HEAD, HEAD), s_map),
    ] + [_const_spec(w.shape) for w in weights]
    in_specs[4 + 2] = _const_spec((d, 8 * d))
    out_shape = (
        jax.ShapeDtypeStruct((n, t_len, d), F32),
        jax.ShapeDtypeStruct((n, SUBLANES, d), F32),
        jax.ShapeDtypeStruct((n, SUBLANES, 3 * d), F32),
        jax.ShapeDtypeStruct((n, n_heads, HEAD, HEAD), F32),
    )
    out_specs = (
        pl.BlockSpec((1, tile, d), lambda b, t: (b, t, 0)),
        pl.BlockSpec((1, SUBLANES, d), lambda b, t: (b, 0, 0)),
        pl.BlockSpec((1, SUBLANES, 3 * d), lambda b, t: (b, 0, 0)),
        pl.BlockSpec((1, n_heads, HEAD, HEAD), lambda b, t: (b, 0, 0, 0)),
    )
    scratch = [
        pltpu.VMEM((SUBLANES + tile, d), F32),
        pltpu.VMEM((SUBLANES + tile, 3 * d), F32),
        pltpu.VMEM((tile, 3 * d), F32),
        pltpu.VMEM((tile, d), F32),
    ]
    return pl.pallas_call(
        functools.partial(_seq_kernel, chunk=chunk),
        grid=(n, t_len // tile),
        in_specs=in_specs, out_specs=out_specs, out_shape=out_shape, scratch_shapes=scratch,
        compiler_params=pltpu.CompilerParams(dimension_semantics=("arbitrary", "arbitrary"),
                                             vmem_limit_bytes=56 * 1024 * 1024),
    )(x, ca0, cq0, s0, *weights)


SEQ_BLOCK = SUBLANES
PAIR_ROWS = 4 * SUBLANES


def _blocked_conv(prefix_ref, new, w_ref, out_ref, tail_ref, *, width, t_len, act):
    pw = (width - 1) * SEQ_BLOCK
    br = t_len * SEQ_BLOCK
    for j in range(new.shape[0] // br):
        ext = jnp.concatenate([prefix_ref[j * pw:(j + 1) * pw, :], new[j * br:(j + 1) * br, :]], axis=0)
        conv = ext[pw:pw + br] * w_ref[width - 1:width, :]
        for tap in range(width - 1):
            conv = conv + ext[tap * SEQ_BLOCK:tap * SEQ_BLOCK + br] * w_ref[tap:tap + 1, :]
        out_ref[j * br:(j + 1) * br, :] = act(conv)
        tail_ref[j * pw:(j + 1) * pw, :] = ext[br:br + pw]


def _time_slab(a, t, t_len, lanes=slice(None)):
    n_blocks = a.shape[0] // (t_len * SEQ_BLOCK)
    return jnp.concatenate([a[(j * t_len + t) * SEQ_BLOCK:(j * t_len + t + 1) * SEQ_BLOCK, lanes]
                            for j in range(n_blocks)], axis=0)


def _sample_pre_kernel(x_ref, cq0_ref, npre_ref, wq_ref, wk_ref, wv_ref, wbeta_ref, walpha_ref, cqw_ref,
                       alog_ref, dtb_ref, cq_ref, pre_ref, qkv_scr, *, t_len):
    rows, d = x_ref.shape
    n_heads = d // HEAD
    n_blocks = rows // (t_len * SEQ_BLOCK)
    n_seq = n_blocks * SEQ_BLOCK
    assert 2 * t_len == SUBLANES
    xn = _rms(x_ref[...], npre_ref[...]).astype(BF16)
    qkv_pre = jnp.concatenate([jnp.dot(xn, w[...], preferred_element_type=F32) for w in (wq_ref, wk_ref, wv_ref)],
                              axis=1)
    _blocked_conv(cq0_ref, qkv_pre, cqw_ref, qkv_scr, cq_ref, width=CONV_QKV_WIDTH, t_len=t_len, act=_silu)

    beta_full = _sigmoid(jnp.dot(xn, wbeta_ref[...], preferred_element_type=F32))
    g_full = -jnp.exp(alog_ref[...]) * _softplus(jnp.dot(xn, walpha_ref[...], preferred_element_type=F32)
                                                 + dtb_ref[...])
    beta = [_time_slab(beta_full, t, t_len) for t in range(t_len)]
    g_cum = []
    for t in range(t_len):
        g_t = _time_slab(g_full, t, t_len)
        g_cum.append(g_t if t == 0 else g_cum[-1] + g_t)
    e_g = [jnp.exp(g) for g in g_cum]
    decay = {(i, j): jnp.exp(g_cum[i] - g_cum[j]) for i in range(t_len) for j in range(i + 1)}
    k_decay = [jnp.exp(g_cum[-1] - g) for g in g_cum]
    zeros = jnp.zeros((n_seq, HEAD), F32)

    for h in range(n_heads):
        def col(a):
            return a[:, h:h + 1]
        q = [_time_slab(qkv_scr, t, t_len, slice(h * HEAD, (h + 1) * HEAD)) for t in range(t_len)]
        k = [_time_slab(qkv_scr, t, t_len, slice(d + h * HEAD, d + (h + 1) * HEAD)) for t in range(t_len)]
        v = [_time_slab(qkv_scr, t, t_len, slice(2 * d + h * HEAD, 2 * d + (h + 1) * HEAD)) for t in range(t_len)]
        q = [a * (lax.rsqrt(jnp.sum(a * a, axis=-1, keepdims=True) + EPS) * (HEAD ** -0.5)) for a in q]
        k = [a * lax.rsqrt(jnp.sum(a * a, axis=-1, keepdims=True) + EPS) for a in k]
        kb = [k[t] * col(beta[t]) for t in range(t_len)]
        a_low = {(i, j): jnp.sum(kb[i] * k[j], axis=-1, keepdims=True) * col(decay[i, j])
                 for i in range(t_len) for j in range(i)}
        attn = {(i, j): jnp.sum(q[i] * k[j], axis=-1, keepdims=True) * col(decay[i, j])
                for i in range(t_len) for j in range(i + 1)}
        t_inv = {}
        for i in range(t_len):
            for j in range(i):
                acc = a_low[i, j]
                for m in range(j + 1, i):
                    acc = acc + a_low[i, m] * t_inv[m, j]
                t_inv[i, j] = -acc
        vb = [v[t] * col(beta[t]) for t in range(t_len)]
        kbe = [kb[t] * col(e_g[t]) for t in range(t_len)]
        value, k_cum = [], []
        for i in range(t_len):
            val, kc = vb[i], kbe[i]
            for j in range(i):
                val = val + t_inv[i, j] * vb[j]
                kc = kc + t_inv[i, j] * kbe[j]
            value.append(val)
            k_cum.append(kc)
        q_eff, o_loc = [], []
        for i in range(t_len):
            qe = q[i] * col(e_g[i])
            ol = None
            for j in range(i + 1):
                qe = qe - attn[i, j] * k_cum[j]
                ol = attn[i, j] * value[j] if ol is None else ol + attn[i, j] * value[j]
            q_eff.append(qe)
            o_loc.append(ol)
        k_dec = [k[t] * col(k_decay[t]) for t in range(t_len)]
        last_decay = jnp.broadcast_to(col(e_g[-1]), (n_seq, HEAD))
        slabs = (q_eff + k_cum + o_loc + value + [zeros] * t_len + k_dec
                 + [last_decay] + [zeros] * (SUBLANES - 1))
        for group, slab in enumerate(slabs):
            for j in range(n_blocks):
                base = j * PAIR_ROWS * SEQ_BLOCK + group * SEQ_BLOCK
                pre_ref[h, base:base + SEQ_BLOCK, :] = slab[j * SEQ_BLOCK:(j + 1) * SEQ_BLOCK]


def _sample_state_kernel(pre_ref, s_ref, o_ref, s_out_ref):
    n_heads = pre_ref.shape[0]
    group = SUBLANES * SEQ_BLOCK
    row = lax.broadcasted_iota(jnp.int32, (SUBLANES, HEAD), 0)
    sign = jnp.where(row < SUBLANES // 2, 1.0, -1.0).astype(F32)
    def seq_rows(first):
        return pl.ds(first, SUBLANES, stride=SEQ_BLOCK)

    for i in range(SEQ_BLOCK):
        prods = [_dot(pre_ref.at[h][seq_rows(i), :], s_ref[i, h]) for h in range(n_heads)]
        for h in range(n_heads):
            o_vnew = pre_ref.at[h][seq_rows(group + i), :] + prods[h] * sign
            k_dec = pre_ref.at[h][seq_rows(2 * group + i), :]
            last_decay = pre_ref[h, 3 * group + i:3 * group + i + 1, :]
            o_ref.at[h][seq_rows(i), :] = o_vnew
            s_out_ref[i, h] = s_ref[i, h] * last_decay + _dot_tn(k_dec, o_vnew)


def _sample_post_kernel(x_ref, ca0_ref, o_ref, npre_ref, npost_ref, wh_ref, wb_ref, wc_ref, wz_ref, wzb_ref,
                        wga_ref, wgb_ref, bgate_ref, caw_ref, gnw_ref, wa_ref, wbo_ref, wo_ref,
                        y_ref, ca_ref, conv_scr, *, t_len):
    rows, d = x_ref.shape
    n_heads = d // HEAD
    br = t_len * SEQ_BLOCK
    n_blocks = rows // br
    x = x_ref[...]
    xn = _rms(x, npre_ref[...]).astype(BF16)

    def proj(w_ref):
        return jnp.dot(xn, w_ref[...], preferred_element_type=F32)

    _blocked_conv(ca0_ref, proj(wc_ref) * proj(wh_ref), caw_ref, conv_scr, ca_ref, width=CONV_A_WIDTH, t_len=t_len,
                  act=lambda a: a)
    y_a = proj(wb_ref) * conv_scr[...] * _silu(proj(wz_ref))
    merged = _sigmoid(proj(wga_ref) + bgate_ref[:, 0:d]) * _dot(y_a, wa_ref[...])
    group = SUBLANES * SEQ_BLOCK
    o_norm = [_rms(jnp.concatenate([o_ref[h, j * group:j * group + br, :] for j in range(n_blocks)], axis=0),
                   gnw_ref[...]) for h in range(n_heads)]
    y_b = jnp.concatenate(o_norm, axis=1) * _silu(proj(wzb_ref))
    merged = merged + _sigmoid(proj(wgb_ref) + bgate_ref[:, d:2 * d]) * _dot(y_b, wbo_ref[...])
    y_ref[...] = x + _rms(_dot(merged, wo_ref[...]), npost_ref[...])


def _to_blocks(a):
    n, t, c = a.shape
    return a.reshape(n // SEQ_BLOCK, SEQ_BLOCK, t, c).transpose(0, 2, 1, 3).reshape(n * t, c)


def _from_blocks(a, t):
    n = a.shape[0] // t
    return a.reshape(n // SEQ_BLOCK, t, SEQ_BLOCK, a.shape[-1]).transpose(0, 2, 1, 3).reshape(n, t, a.shape[-1])


def _sample_layer(x, ca0, cq0, s0, w):
    n, t_len, d = x.shape
    n_heads = d // HEAD
    n_blocks = n // SEQ_BLOCK
    tile_blocks = min(8, n_blocks)
    n_tiles = n_blocks // tile_blocks
    rows = tile_blocks * t_len * SEQ_BLOCK
    xb, cab, cqb = _to_blocks(x), _to_blocks(ca0), _to_blocks(cq0)
    wm = w["w_main"]

    def col_block(k):
        return pl.BlockSpec((d, d), lambda t: (0, k))

    def whole(a):
        nd = a.ndim
        return pl.BlockSpec(a.shape, lambda t: (0,) * nd)

    def row_tile(r, c):
        return pl.BlockSpec((r, c), lambda t: (t, 0))

    params = pltpu.CompilerParams(dimension_semantics=("arbitrary",), vmem_limit_bytes=56 * 1024 * 1024)
    pa_rows = (CONV_A_WIDTH - 1) * SEQ_BLOCK * tile_blocks
    pq_rows = (CONV_QKV_WIDTH - 1) * SEQ_BLOCK * tile_blocks
    pre_rows = PAIR_ROWS * SEQ_BLOCK

    cq_new, pre = pl.pallas_call(
        functools.partial(_sample_pre_kernel, t_len=t_len),
        grid=(n_tiles,),
        in_specs=[row_tile(rows, d), row_tile(pq_rows, 3 * d), whole(w["norm_pre"]),
                  col_block(4), col_block(5), col_block(6), whole(w["w_beta"]), whole(w["w_alpha"]),
                  whole(w["conv_qkv_w"]), whole(w["a_log0"]), whole(w["dt_bias0"])],
        out_specs=(row_tile(pq_rows, 3 * d),
                   pl.BlockSpec((n_heads, tile_blocks * pre_rows, HEAD), lambda t: (0, t, 0))),
        out_shape=(jax.ShapeDtypeStruct(cqb.shape, F32),
                   jax.ShapeDtypeStruct((n_heads, n_blocks * pre_rows, HEAD), F32)),
        scratch_shapes=[pltpu.VMEM((rows, 3 * d), F32)],
        compiler_params=params,
    )(xb, cqb, w["norm_pre"], wm, wm, wm, w["w_beta"], w["w_alpha"], w["conv_qkv_w"], w["a_log0"], w["dt_bias0"])

    o_rows = SUBLANES * SEQ_BLOCK
    o_blk, s_new = pl.pallas_call(
        _sample_state_kernel,
        grid=(n_blocks,),
        in_specs=[pl.BlockSpec((n_heads, pre_rows, HEAD), lambda j: (0, j, 0)),
                  pl.BlockSpec((SEQ_BLOCK, n_heads, HEAD, HEAD), lambda j: (j, 0, 0, 0))],
        out_specs=(pl.BlockSpec((n_heads, o_rows, HEAD), lambda j: (0, j, 0)),
                   pl.BlockSpec((SEQ_BLOCK, n_heads, HEAD, HEAD), lambda j: (j, 0, 0, 0))),
        out_shape=(jax.ShapeDtypeStruct((n_heads, n_blocks * o_rows, HEAD), F32),
                   jax.ShapeDtypeStruct(s0.shape, F32)),
        compiler_params=params,
    )(pre, s0)

    y, ca_new = pl.pallas_call(
        functools.partial(_sample_post_kernel, t_len=t_len),
        grid=(n_tiles,),
        in_specs=[row_tile(rows, d), row_tile(pa_rows, d),
                  pl.BlockSpec((n_heads, tile_blocks * o_rows, HEAD), lambda t: (0, t, 0)),
                  whole(w["norm_pre"]), whole(w["norm_post"]),
                  col_block(0), col_block(1), col_block(2), col_block(3), col_block(7), col_block(0), col_block(1),
                  whole(w["b_gate"]), whole(w["conv_a_w"]), whole(w["gnorm_w"]),
                  whole(w["w_a_out"]), whole(w["w_b_out"]), whole(w["w_o"])],
        out_specs=(row_tile(rows, d), row_tile(pa_rows, d)),
        out_shape=(jax.ShapeDtypeStruct(xb.shape, F32), jax.ShapeDtypeStruct(cab.shape, F32)),
        scratch_shapes=[pltpu.VMEM((rows, d), F32)],
        compiler_params=params,
    )(xb, cab, o_blk, w["norm_pre"], w["norm_post"], wm, wm, wm, wm, wm, w["w_gates"], w["w_gates"],
      w["b_gate"], w["conv_a_w"], w["gnorm_w"], w["w_a_out"], w["w_b_out"], w["w_o"])

    return (_from_blocks(y, t_len), _from_blocks(ca_new, CONV_A_WIDTH - 1),
            _from_blocks(cq_new, CONV_QKV_WIDTH - 1), s_new)


def _tail_rows(state, width):
    n, r, c = state.shape
    return jnp.concatenate([jnp.zeros((n, SUBLANES - r, c), state.dtype), state], axis=1)


def _lane_row(vec, offset):
    return jnp.zeros((1, HEAD), F32).at[0, offset:offset + vec.shape[0]].set(vec.astype(F32))


def kernel(x_prompt, x_sample, state_conv_a, state_conv_qkv, state_delta, meta, norm_pre, norm_post, w_in, b_gate,
           conv_a_w, conv_qkv_w, a_log, dt_bias, gnorm_w, w_a_out, w_b_out, w_o):
    depth = w_in.shape[0]
    assert depth == 1, "single-layer trunk"
    d = x_prompt.shape[-1]
    n_heads = d // HEAD
    bsz = x_prompt.shape[0]
    l = 0
    w = w_in[l]
    o_small = 8 * d
    w16 = w.astype(BF16)
    w_gates = w16[:, o_small + 2 * n_heads:]
    w_ab = jnp.concatenate([w16[:, o_small:o_small + 2 * n_heads],
                            jnp.zeros((d, HEAD - 2 * n_heads), BF16)], axis=1)
    weights = (
        norm_pre[l][None, :], norm_post[l][None, :], w16, w_gates, w_ab, b_gate[l][None, :],
        conv_a_w[l], conv_qkv_w[l], _lane_row(a_log[l], n_heads), _lane_row(dt_bias[l], n_heads),
        gnorm_w[l][None, :], w_a_out[l].astype(BF16), w_b_out[l].astype(BF16), w_o[l].astype(BF16),
    )

    zeros_a = jnp.zeros((1, SUBLANES, d), F32)
    zeros_q = jnp.zeros((1, SUBLANES, 3 * d), F32)
    zeros_s = jnp.zeros((1, n_heads, HEAD, HEAD), F32)
    _, ca_m, cq_m, s_m = _seq_layer(meta[None].astype(F32), zeros_a, zeros_q, zeros_s, weights,
                                    tile=N_META, chunk=N_META, shared_state=True)
    tile = min(256, x_prompt.shape[1])
    y_p, ca_p, cq_p, s_p = _seq_layer(x_prompt, ca_m, cq_m, s_m, weights,
                                      tile=tile, chunk=CHUNK, shared_state=True)
    def small_w(cols):
        return jnp.concatenate([cols, jnp.zeros((d, HEAD - n_heads), BF16)], axis=1)

    sample_w = dict(
        norm_pre=weights[0], norm_post=weights[1], w_main=w16, w_gates=w_gates, b_gate=weights[5],
        w_beta=small_w(w16[:, o_small:o_small + n_heads]),
        w_alpha=small_w(w16[:, o_small + n_heads:o_small + 2 * n_heads]),
        conv_a_w=conv_a_w[l], conv_qkv_w=conv_qkv_w[l], a_log0=_lane_row(a_log[l], 0),
        dt_bias0=_lane_row(dt_bias[l], 0), gnorm_w=weights[10], w_a_out=weights[11], w_b_out=weights[12],
        w_o=weights[13],
    )
    y_s, ca_s, cq_s, s_s = _sample_layer(x_sample, state_conv_a[l], state_conv_qkv[l], state_delta[l], sample_w)

    def tails(c, width):
        return c[None, :, SUBLANES - (width - 1):, :]

    return (y_p, y_s, tails(ca_p, CONV_A_WIDTH), tails(cq_p, CONV_QKV_WIDTH), s_p[None],
            ca_s[None], cq_s[None], s_s[None])
```

```python
import functools
import math

import jax
import jax.numpy as jnp
from jax import lax
from jax.experimental import pallas as pl
from jax.experimental.pallas import tpu as pltpu

HEAD = 128
CONV_A_WIDTH = 3
CONV_QKV_WIDTH = 4
CHUNK = 64
N_META = 16
EPS = 1e-6
SUBLANES = 8
BF16 = jnp.bfloat16
F32 = jnp.float32


def _dot(a, b):
    return jnp.dot(a.astype(BF16), b.astype(BF16), preferred_element_type=F32)


def _dot_nt(a, b):
    return lax.dot_general(a.astype(BF16), b.astype(BF16), (((1,), (1,)), ((), ())),
                           preferred_element_type=F32)


def _dot_tn(a, b):
    return lax.dot_general(a.astype(BF16), b.astype(BF16), (((0,), (0,)), ((), ())),
                           preferred_element_type=F32)


def _sigmoid(x):
    return 1.0 / (1.0 + jnp.exp(-x))


def _silu(x):
    return x * _sigmoid(x)


def _softplus(x):
    return jnp.maximum(x, 0.0) + jnp.log1p(jnp.exp(-jnp.abs(x)))


def _rms(x, w):
    return x * lax.rsqrt(jnp.mean(x * x, axis=-1, keepdims=True) + EPS) * w


def _chunk_cumsum(x, chunk):
    row = lax.broadcasted_iota(jnp.int32, x.shape, 0) & (chunk - 1)
    sh = 1
    while sh < chunk:
        x = x + jnp.where(row >= sh, pltpu.roll(x, sh, 0), 0.0)
        sh *= 2
    return x


def _pad_lanes(piece, before, after):
    parts = [piece]
    if before:
        parts.insert(0, jnp.zeros((piece.shape[0], before), piece.dtype))
    if after:
        parts.append(jnp.zeros((piece.shape[0], after), piece.dtype))
    return jnp.concatenate(parts, axis=1) if len(parts) > 1 else piece


def _block_diag(cat):
    chunk, tt = cat.shape
    n = tt // chunk
    cat = cat.astype(BF16)
    if n == 1:
        return cat
    width = min(tt, HEAD)
    per_width = width // chunk
    lane_piece = lax.broadcasted_iota(jnp.int32, (chunk, width), 1) >> int(math.log2(chunk))
    rows = []
    for c in range(n):
        v, p = divmod(c, per_width)
        tile = cat[:, v * width:(v + 1) * width]
        if per_width > 1:
            tile = jnp.where(lane_piece == p, tile, jnp.zeros((), BF16))
        rows.append(_pad_lanes(tile, v * width, tt - (v + 1) * width))
    return jnp.concatenate(rows, axis=0)


def _delta_context(g_all, tt, chunk):
    ci = lax.broadcasted_iota(jnp.int32, (chunk, tt), 0)
    cj = lax.broadcasted_iota(jnp.int32, (chunk, tt), 1) & (chunk - 1)
    g_cum_all = _chunk_cumsum(g_all, chunk)
    return dict(incl_cat=ci >= cj, strict_cat=ci > cj,
                eye_cat=jnp.where(ci == cj, 1.0, 0.0).astype(F32),
                g_cum_all=g_cum_all, g_cum_t=g_cum_all.T, e_g_all=jnp.exp(g_cum_all))


def _column_per_chunk(col, chunk):
    tt = col.shape[0]
    width = min(tt, HEAD)
    per_width = width // chunk
    full = jnp.broadcast_to(col, (tt, width))
    lane_piece = lax.broadcasted_iota(jnp.int32, (chunk, width), 1) >> int(math.log2(chunk))
    parts = []
    for v in range(tt // width):
        acc = full[v * per_width * chunk:(v * per_width + 1) * chunk]
        for p in range(1, per_width):
            c = v * per_width + p
            acc = jnp.where(lane_piece == p, full[c * chunk:(c + 1) * chunk], acc)
        parts.append(acc)
    return jnp.concatenate(parts, axis=1)


def _chunks_side_by_side(a, chunk):
    return jnp.concatenate([a[c * chunk:(c + 1) * chunk] for c in range(a.shape[0] // chunk)], axis=1)


def _delta_tile(ctx, qkv_scr, beta_all, s_ref, o_scr, gnw, heads, *, chunk):
    tt = qkv_scr.shape[0]
    d = qkv_scr.shape[1] // 3
    n_heads = d // HEAD
    n_c = tt // chunk
    shift = int(math.log2(chunk))
    eye_cat = ctx["eye_cat"]
    incl, strict = ctx["incl_cat"], ctx["strict_cat"]
    g_cum_all, g_cum_t, e_g_all = ctx["g_cum_all"], ctx["g_cum_t"], ctx["e_g_all"]

    qs, ks, rhs, gcs, attns, ms, ps = [], [], [], [], [], [], []
    for i, h in enumerate(heads):
        q = qkv_scr[:, h * HEAD:(h + 1) * HEAD]
        k = qkv_scr[:, d + h * HEAD:d + (h + 1) * HEAD]
        v = qkv_scr[:, 2 * d + h * HEAD:2 * d + (h + 1) * HEAD]
        q = q * (lax.rsqrt(jnp.sum(q * q, axis=-1, keepdims=True) + EPS) * (HEAD ** -0.5))
        k = k * lax.rsqrt(jnp.sum(k * k, axis=-1, keepdims=True) + EPS)
        beta = beta_all[:, h:h + 1]
        g_cum = g_cum_all[:, n_heads + h:n_heads + h + 1]
        e_g = e_g_all[:, n_heads + h:n_heads + h + 1]
        diff = _column_per_chunk(g_cum, chunk) - g_cum_t[n_heads + h:n_heads + h + 1, :]
        decay = jnp.where(incl, jnp.exp(jnp.where(incl, diff, 0.0)), 0.0)
        kb = k * beta
        k16 = k.astype(BF16)
        k_bd = jnp.concatenate([_pad_lanes(k16[c * chunk:(c + 1) * chunk], c * HEAD, (n_c - 1 - c) * HEAD)
                                for c in range(n_c)], axis=0)
        kq = _dot_nt(jnp.concatenate([_chunks_side_by_side(kb, chunk), _chunks_side_by_side(q, chunk)], axis=0),
                     k_bd)
        m = -jnp.where(strict, kq[:chunk] * decay, 0.0)
        attns.append(_block_diag(kq[chunk:] * decay))
        ms.append(m)
        ps.append(eye_cat + m)
        qs.append(q * e_g)
        ks.append(k)
        rhs.append(jnp.concatenate([v * beta, kb * e_g], axis=1).astype(BF16))
        gcs.append(g_cum)
        if i % 2 == 1:
            yield

    ms = [_dot(m, _block_diag(m)) for m in ms]
    yield
    for k in range(1, shift):
        last = k == shift - 1
        for h in range(len(heads)):
            m_bd = _block_diag(ms[h])
            if last:
                ps[h] = ps[h] + _dot(ps[h], m_bd)
            else:
                prod = _dot(jnp.concatenate([ps[h], ms[h]], axis=0), m_bd)
                ps[h] = ps[h] + prod[:chunk]
                ms[h] = prod[chunk:]
        yield
    vks = [_dot(_block_diag(p), r) for p, r in zip(ps, rhs)]
    yield

    q_effs, o_locs, wbs, last_decays = [], [], [], []
    for h in range(len(heads)):
        av = _dot(attns[h], vks[h])
        o_locs.append(av[:, :HEAD])
        q_effs.append(qs[h] - av[:, HEAD:])
        wb_h, dl_h = [], []
        for c in range(n_c):
            rows = slice(c * chunk, (c + 1) * chunk)
            g_cum = gcs[h][rows]
            g_last = g_cum[chunk - 1:chunk, :]
            k_dec = ks[h][rows] * jnp.exp(g_last - g_cum)
            wb_h.append(_dot_tn(k_dec, jnp.concatenate([vks[h][rows, HEAD:], vks[h][rows, :HEAD]], axis=1)))
            dl_h.append(jnp.exp(g_last))
        wbs.append(wb_h)
        last_decays.append(dl_h)
        if h % 2 == 1:
            yield

    o_parts = [[] for _ in heads]
    for c in range(n_c):
        rows = slice(c * chunk, (c + 1) * chunk)
        for i, h in enumerate(heads):
            s = s_ref[0, h]
            prod = _dot(jnp.concatenate([wbs[i][c][:, :HEAD], q_effs[i][rows]], axis=0), s)
            s_ref[0, h] = s * last_decays[i][c] - prod[:HEAD] + wbs[i][c][:, HEAD:]
            o_parts[i].append(prod[HEAD:] + o_locs[i][rows])
        yield
    for i, h in enumerate(heads):
        o_scr[:, h * HEAD:(h + 1) * HEAD] = _rms(jnp.concatenate(o_parts[i], axis=0), gnw)


def _to_lane_tiles(buf, row0, value, tile0=0):
    for j in range(value.shape[1] // HEAD):
        buf[tile0 + j, row0:row0 + value.shape[0], :] = value[:, j * HEAD:(j + 1) * HEAD]


def _from_lane_tiles(buf, row0, n_rows):
    return jnp.concatenate([buf[j, row0:row0 + n_rows, :] for j in range(buf.shape[0])], axis=1)


def _causal_conv(buf, w_ref, j):
    width = w_ref.shape[0]
    tt = buf.shape[1] - SUBLANES
    lanes = slice(j * HEAD, (j + 1) * HEAD)
    out = buf[j, SUBLANES:SUBLANES + tt, :] * w_ref[width - 1:width, lanes]
    for t in range(width - 1):
        off = SUBLANES - (width - 1) + t
        out = out + buf[j, off:off + tt, :] * w_ref[t:t + 1, lanes]
    return out


_EXHAUSTED = object()


def _alternate(first, second, lead):
    for _ in range(lead):
        next(first, None)
        yield
    live = [first, second]
    while live:
        still = []
        for g in live:
            if next(g, _EXHAUSTED) is not _EXHAUSTED:
                still.append(g)
                yield
        live = still


def _seq_kernel(x_ref, ca0_ref, cq0_ref, s0_ref, npre_ref, npost_ref, wmain_ref, wgate_ref, wab_ref, bgate_ref,
                caw_ref, cqw_ref, alog_ref, dtb_ref, gnw_ref, wa_ref, wb_ref, wo_ref,
                y_ref, ca_ref, cq_ref, s_ref,
                ubuf, qbuf, qkv_scr, o_scr, *, chunk):
    tt, d = x_ref.shape[1], x_ref.shape[2]
    t_idx = pl.program_id(1)

    @pl.when(t_idx == 0)
    def _():
        _to_lane_tiles(ubuf, 0, ca0_ref[0])
        _to_lane_tiles(qbuf, 0, cq0_ref[0])
        s_ref[0] = s0_ref[0]

    @pl.when(t_idx > 0)
    def _():
        ubuf[:, 0:SUBLANES, :] = ubuf[:, tt:tt + SUBLANES, :]
        qbuf[:, 0:SUBLANES, :] = qbuf[:, tt:tt + SUBLANES, :]

    x = x_ref[0]
    xn = _rms(x, npre_ref[...]).astype(BF16)

    def proj(k):
        return jnp.dot(xn, wmain_ref[:, k * d:(k + 1) * d], preferred_element_type=F32)

    def branch_b():
        for k in range(3):
            _to_lane_tiles(qbuf, SUBLANES, proj(4 + k), tile0=k * (d // HEAD))
            yield
        for j in range(3 * d // HEAD):
            qkv_scr[:, j * HEAD:(j + 1) * HEAD] = _silu(_causal_conv(qbuf, cqw_ref, j))
        cq_ref[0] = _from_lane_tiles(qbuf, tt, SUBLANES)
        ab = jnp.dot(xn, wab_ref[...], preferred_element_type=F32)
        beta_all = _sigmoid(ab)
        g_all = -jnp.exp(alog_ref[...]) * _softplus(ab + dtb_ref[...])
        yield
        ctx = _delta_context(g_all, tt, chunk)
        n_heads = d // HEAD
        groups = [_delta_tile(ctx, qkv_scr, beta_all, s_ref, o_scr, gnw_ref[...], heads, chunk=chunk)
                  for heads in (range(0, n_heads // 2), range(n_heads // 2, n_heads))]
        yield from _alternate(groups[0], groups[1], lead=2)

    side = {}

    def halves(lhs, w_ref, col0):
        lo = jnp.dot(lhs, w_ref[:, col0:col0 + d // 2], preferred_element_type=F32)
        yield
        hi = jnp.dot(lhs, w_ref[:, col0 + d // 2:col0 + d], preferred_element_type=F32)
        return jnp.concatenate([lo, hi], axis=1)

    def branch_a():
        h_a = yield from halves(xn, wmain_ref, 0)
        yield
        c_a = yield from halves(xn, wmain_ref, 2 * d)
        _to_lane_tiles(ubuf, SUBLANES, c_a * h_a)
        yield
        conv_u = jnp.concatenate([_causal_conv(ubuf, caw_ref, j) for j in range(d // HEAD)], axis=1)
        ca_ref[0] = _from_lane_tiles(ubuf, tt, SUBLANES)
        b_a = yield from halves(xn, wmain_ref, d)
        y_a = b_a * conv_u
        yield
        z_a = yield from halves(xn, wmain_ref, 3 * d)
        y_a = (y_a * _silu(z_a)).astype(BF16)
        yield
        gate_a = yield from halves(xn, wgate_ref, 0)
        gate_a = _sigmoid(gate_a + bgate_ref[:, 0:d])
        yield
        ya_p = yield from halves(y_a, wa_ref, 0)
        side["merged"] = gate_a * ya_p
        yield
        z_b = yield from halves(xn, wmain_ref, 7 * d)
        side["silu_z_b"] = _silu(z_b)
        yield
        gate_b = yield from halves(xn, wgate_ref, d)
        side["gate_b"] = _sigmoid(gate_b + bgate_ref[:, d:2 * d])

    for _ in _alternate(branch_b(), branch_a(), lead=2):
        pass

    y_b = o_scr[...] * side["silu_z_b"]
    merged = side["merged"] + side["gate_b"] * _dot(y_b, wb_ref[...])
    y_ref[0] = x + _rms(_dot(merged, wo_ref[...]), npost_ref[...])


def _const_spec(shape):
    nd = len(shape)
    return pl.BlockSpec(shape, lambda b, t: (0,) * nd)


def _seq_layer(x, ca0, cq0, s0, weights, *, tile, chunk, shared_state):
    n, t_len, d = x.shape
    n_heads = d // HEAD
    state_map = (lambda b, t: (0, 0, 0)) if shared_state else (lambda b, t: (b, 0, 0))
    s_map = (lambda b, t: (0, 0, 0, 0)) if shared_state else (lambda b, t: (b, 0, 0, 0))
    in_specs = [
        pl.BlockSpec((1, tile, d), lambda b, t: (b, t, 0)),
        pl.BlockSpec((1, SUBLANES, d), state_map),
        pl.BlockSpec((1, SUBLANES, 3 * d), state_map),
        pl.BlockSpec((1, n_heads, HEAD, HEAD), s_map),
    ] + [_const_spec(w.shape) for w in weights]
    in_specs[4 + 2] = _const_spec((d, 8 * d))
    out_shape = (
        jax.ShapeDtypeStruct((n, t_len, d), F32),
        jax.ShapeDtypeStruct((n, SUBLANES, d), F32),
        jax.ShapeDtypeStruct((n, SUBLANES, 3 * d), F32),
        jax.ShapeDtypeStruct((n, n_heads, HEAD, HEAD), F32),
    )
    out_specs = (
        pl.BlockSpec((1, tile, d), lambda b, t: (b, t, 0)),
        pl.BlockSpec((1, SUBLANES, d), lambda b, t: (b, 0, 0)),
        pl.BlockSpec((1, SUBLANES, 3 * d), lambda b, t: (b, 0, 0)),
        pl.BlockSpec((1, n_heads, HEAD, HEAD), lambda b, t: (b, 0, 0, 0)),
    )
    scratch = [
        pltpu.VMEM((d // HEAD, SUBLANES + tile, HEAD), F32),
        pltpu.VMEM((3 * d // HEAD, SUBLANES + tile, HEAD), F32),
        pltpu.VMEM((tile, 3 * d), F32),
        pltpu.VMEM((tile, d), F32),
    ]
    return pl.pallas_call(
        functools.partial(_seq_kernel, chunk=chunk),
        grid=(n, t_len // tile),
        in_specs=in_specs, out_specs=out_specs, out_shape=out_shape, scratch_shapes=scratch,
        compiler_params=pltpu.CompilerParams(dimension_semantics=("arbitrary", "arbitrary"),
                                             vmem_limit_bytes=56 * 1024 * 1024),
    )(x, ca0, cq0, s0, *weights)


SEQ_BLOCK = SUBLANES
PAIR_ROWS = 4 * SUBLANES


def _blocked_conv(prefix_ref, new, w_ref, out_ref, tail_ref, *, width, t_len, act):
    pw = (width - 1) * SEQ_BLOCK
    br = t_len * SEQ_BLOCK
    for j in range(new.shape[0] // br):
        ext = jnp.concatenate([prefix_ref[j * pw:(j + 1) * pw, :], new[j * br:(j + 1) * br, :]], axis=0)
        conv = ext[pw:pw + br] * w_ref[width - 1:width, :]
        for tap in range(width - 1):
            conv = conv + ext[tap * SEQ_BLOCK:tap * SEQ_BLOCK + br] * w_ref[tap:tap + 1, :]
        out_ref[j * br:(j + 1) * br, :] = act(conv)
        tail_ref[j * pw:(j + 1) * pw, :] = ext[br:br + pw]


def _time_slab(a, t, t_len, lanes=slice(None)):
    n_blocks = a.shape[0] // (t_len * SEQ_BLOCK)
    return jnp.concatenate([a[(j * t_len + t) * SEQ_BLOCK:(j * t_len + t + 1) * SEQ_BLOCK, lanes]
                            for j in range(n_blocks)], axis=0)


def _sample_pre_kernel(x_ref, cq0_ref, npre_ref, wq_ref, wk_ref, wv_ref, wbeta_ref, walpha_ref, cqw_ref,
                       alog_ref, dtb_ref, cq_ref, pre_ref, qkv_scr, *, t_len):
    rows, d = x_ref.shape
    n_heads = d // HEAD
    n_blocks = rows // (t_len * SEQ_BLOCK)
    n_seq = n_blocks * SEQ_BLOCK
    assert 2 * t_len == SUBLANES
    xn = _rms(x_ref[...], npre_ref[...]).astype(BF16)
    qkv_pre = jnp.concatenate([jnp.dot(xn, w[...], preferred_element_type=F32) for w in (wq_ref, wk_ref, wv_ref)],
                              axis=1)
    _blocked_conv(cq0_ref, qkv_pre, cqw_ref, qkv_scr, cq_ref, width=CONV_QKV_WIDTH, t_len=t_len, act=_silu)

    beta_full = _sigmoid(jnp.dot(xn, wbeta_ref[...], preferred_element_type=F32))
    g_full = -jnp.exp(alog_ref[...]) * _softplus(jnp.dot(xn, walpha_ref[...], preferred_element_type=F32)
                                                 + dtb_ref[...])
    beta = [_time_slab(beta_full, t, t_len) for t in range(t_len)]
    g_cum = []
    for t in range(t_len):
        g_t = _time_slab(g_full, t, t_len)
        g_cum.append(g_t if t == 0 else g_cum[-1] + g_t)
    e_g = [jnp.exp(g) for g in g_cum]
    decay = {(i, j): jnp.exp(g_cum[i] - g_cum[j]) for i in range(t_len) for j in range(i + 1)}
    k_decay = [jnp.exp(g_cum[-1] - g) for g in g_cum]
    zeros = jnp.zeros((n_seq, HEAD), F32)

    for h in range(n_heads):
        def col(a):
            return a[:, h:h + 1]
        q = [_time_slab(qkv_scr, t, t_len, slice(h * HEAD, (h + 1) * HEAD)) for t in range(t_len)]
        k = [_time_slab(qkv_scr, t, t_len, slice(d + h * HEAD, d + (h + 1) * HEAD)) for t in range(t_len)]
        v = [_time_slab(qkv_scr, t, t_len, slice(2 * d + h * HEAD, 2 * d + (h + 1) * HEAD)) for t in range(t_len)]
        q = [a * (lax.rsqrt(jnp.sum(a * a, axis=-1, keepdims=True) + EPS) * (HEAD ** -0.5)) for a in q]
        k = [a * lax.rsqrt(jnp.sum(a * a, axis=-1, keepdims=True) + EPS) for a in k]
        kb = [k[t] * col(beta[t]) for t in range(t_len)]
        a_low = {(i, j): jnp.sum(kb[i] * k[j], axis=-1, keepdims=True) * col(decay[i, j])
                 for i in range(t_len) for j in range(i)}
        attn = {(i, j): jnp.sum(q[i] * k[j], axis=-1, keepdims=True) * col(decay[i, j])
                for i in range(t_len) for j in range(i + 1)}
        t_inv = {}
        for i in range(t_len):
            for j in range(i):
                acc = a_low[i, j]
                for m in range(j + 1, i):
                    acc = acc + a_low[i, m] * t_inv[m, j]
                t_inv[i, j] = -acc
        vb = [v[t] * col(beta[t]) for t in range(t_len)]
        kbe = [kb[t] * col(e_g[t]) for t in range(t_len)]
        value, k_cum = [], []
        for i in range(t_len):
            val, kc = vb[i], kbe[i]
            for j in range(i):
                val = val + t_inv[i, j] * vb[j]
                kc = kc + t_inv[i, j] * kbe[j]
            value.append(val)
            k_cum.append(kc)
        q_eff, o_loc = [], []
        for i in range(t_len):
            qe = q[i] * col(e_g[i])
            ol = None
            for j in range(i + 1):
                qe = qe - attn[i, j] * k_cum[j]
                ol = attn[i, j] * value[j] if ol is None else ol + attn[i, j] * value[j]
            q_eff.append(qe)
            o_loc.append(ol)
        k_dec = [k[t] * col(k_decay[t]) for t in range(t_len)]
        last_decay = jnp.broadcast_to(col(e_g[-1]), (n_seq, HEAD))
        slabs = (q_eff + k_cum + o_loc + value + [zeros] * t_len + k_dec
                 + [last_decay] + [zeros] * (SUBLANES - 1))
        for group, slab in enumerate(slabs):
            for j in range(n_blocks):
                base = j * PAIR_ROWS * SEQ_BLOCK + group * SEQ_BLOCK
                pre_ref[h, base:base + SEQ_BLOCK, :] = slab[j * SEQ_BLOCK:(j + 1) * SEQ_BLOCK]


def _sample_state_kernel(pre_ref, s_ref, o_ref, s_out_ref):
    n_heads = pre_ref.shape[0]
    group = SUBLANES * SEQ_BLOCK
    row = lax.broadcasted_iota(jnp.int32, (SUBLANES, HEAD), 0)
    sign = jnp.where(row < SUBLANES // 2, 1.0, -1.0).astype(F32)
    def seq_rows(first):
        return pl.ds(first, SUBLANES, stride=SEQ_BLOCK)

    for i in range(SEQ_BLOCK):
        prods = [_dot(pre_ref.at[h][seq_rows(i), :], s_ref[i, h]) for h in range(n_heads)]
        for h in range(n_heads):
            o_vnew = pre_ref.at[h][seq_rows(group + i), :] + prods[h] * sign
            k_dec = pre_ref.at[h][seq_rows(2 * group + i), :]
            last_decay = pre_ref[h, 3 * group + i:3 * group + i + 1, :]
            o_ref.at[h][seq_rows(i), :] = o_vnew
            s_out_ref[i, h] = s_ref[i, h] * last_decay + _dot_tn(k_dec, o_vnew)


def _sample_post_kernel(x_ref, ca0_ref, o_ref, npre_ref, npost_ref, wh_ref, wb_ref, wc_ref, wz_ref, wzb_ref,
                        wga_ref, wgb_ref, bgate_ref, caw_ref, gnw_ref, wa_ref, wbo_ref, wo_ref,
                        y_ref, ca_ref, conv_scr, *, t_len):
    rows, d = x_ref.shape
    n_heads = d // HEAD
    br = t_len * SEQ_BLOCK
    n_blocks = rows // br
    x = x_ref[...]
    xn = _rms(x, npre_ref[...]).astype(BF16)

    def proj(w_ref):
        return jnp.dot(xn, w_ref[...], preferred_element_type=F32)

    _blocked_conv(ca0_ref, proj(wc_ref) * proj(wh_ref), caw_ref, conv_scr, ca_ref, width=CONV_A_WIDTH, t_len=t_len,
                  act=lambda a: a)
    y_a = proj(wb_ref) * conv_scr[...] * _silu(proj(wz_ref))
    merged = _sigmoid(proj(wga_ref) + bgate_ref[:, 0:d]) * _dot(y_a, wa_ref[...])
    group = SUBLANES * SEQ_BLOCK
    o_norm = [_rms(jnp.concatenate([o_ref[h, j * group:j * group + br, :] for j in range(n_blocks)], axis=0),
                   gnw_ref[...]) for h in range(n_heads)]
    y_b = jnp.concatenate(o_norm, axis=1) * _silu(proj(wzb_ref))
    merged = merged + _sigmoid(proj(wgb_ref) + bgate_ref[:, d:2 * d]) * _dot(y_b, wbo_ref[...])
    y_ref[...] = x + _rms(_dot(merged, wo_ref[...]), npost_ref[...])


def _to_blocks(a):
    n, t, c = a.shape
    return a.reshape(n // SEQ_BLOCK, SEQ_BLOCK, t, c).transpose(0, 2, 1, 3).reshape(n * t, c)


def _from_blocks(a, t):
    n = a.shape[0] // t
    return a.reshape(n // SEQ_BLOCK, t, SEQ_BLOCK, a.shape[-1]).transpose(0, 2, 1, 3).reshape(n, t, a.shape[-1])


def _sample_layer(x, ca0, cq0, s0, w):
    n, t_len, d = x.shape
    n_heads = d // HEAD
    n_blocks = n // SEQ_BLOCK
    tile_blocks = min(8, n_blocks)
    n_tiles = n_blocks // tile_blocks
    rows = tile_blocks * t_len * SEQ_BLOCK
    xb, cab, cqb = _to_blocks(x), _to_blocks(ca0), _to_blocks(cq0)
    wm = w["w_main"]

    def col_block(k):
        return pl.BlockSpec((d, d), lambda t: (0, k))

    def whole(a):
        nd = a.ndim
        return pl.BlockSpec(a.shape, lambda t: (0,) * nd)

    def row_tile(r, c):
        return pl.BlockSpec((r, c), lambda t: (t, 0))

    params = pltpu.CompilerParams(dimension_semantics=("arbitrary",), vmem_limit_bytes=56 * 1024 * 1024)
    pa_rows = (CONV_A_WIDTH - 1) * SEQ_BLOCK * tile_blocks
    pq_rows = (CONV_QKV_WIDTH - 1) * SEQ_BLOCK * tile_blocks
    pre_rows = PAIR_ROWS * SEQ_BLOCK

    cq_new, pre = pl.pallas_call(
        functools.partial(_sample_pre_kernel, t_len=t_len),
        grid=(n_tiles,),
        in_specs=[row_tile(rows, d), row_tile(pq_rows, 3 * d), whole(w["norm_pre"]),
                  col_block(4), col_block(5), col_block(6), whole(w["w_beta"]), whole(w["w_alpha"]),
                  whole(w["conv_qkv_w"]), whole(w["a_log0"]), whole(w["dt_bias0"])],
        out_specs=(row_tile(pq_rows, 3 * d),
                   pl.BlockSpec((n_heads, tile_blocks * pre_rows, HEAD), lambda t: (0, t, 0))),
        out_shape=(jax.ShapeDtypeStruct(cqb.shape, F32),
                   jax.ShapeDtypeStruct((n_heads, n_blocks * pre_rows, HEAD), F32)),
        scratch_shapes=[pltpu.VMEM((rows, 3 * d), F32)],
        compiler_params=params,
    )(xb, cqb, w["norm_pre"], wm, wm, wm, w["w_beta"], w["w_alpha"], w["conv_qkv_w"], w["a_log0"], w["dt_bias0"])

    o_rows = SUBLANES * SEQ_BLOCK
    o_blk, s_new = pl.pallas_call(
        _sample_state_kernel,
        grid=(n_blocks,),
        in_specs=[pl.BlockSpec((n_heads, pre_rows, HEAD), lambda j: (0, j, 0)),
                  pl.BlockSpec((SEQ_BLOCK, n_heads, HEAD, HEAD), lambda j: (j, 0, 0, 0))],
        out_specs=(pl.BlockSpec((n_heads, o_rows, HEAD), lambda j: (0, j, 0)),
                   pl.BlockSpec((SEQ_BLOCK, n_heads, HEAD, HEAD), lambda j: (j, 0, 0, 0))),
        out_shape=(jax.ShapeDtypeStruct((n_heads, n_blocks * o_rows, HEAD), F32),
                   jax.ShapeDtypeStruct(s0.shape, F32)),
        compiler_params=params,
    )(pre, s0)

    y, ca_new = pl.pallas_call(
        functools.partial(_sample_post_kernel, t_len=t_len),
        grid=(n_tiles,),
        in_specs=[row_tile(rows, d), row_tile(pa_rows, d),
                  pl.BlockSpec((n_heads, tile_blocks * o_rows, HEAD), lambda t: (0, t, 0)),
                  whole(w["norm_pre"]), whole(w["norm_post"]),
                  col_block(0), col_block(1), col_block(2), col_block(3), col_block(7), col_block(0), col_block(1),
                  whole(w["b_gate"]), whole(w["conv_a_w"]), whole(w["gnorm_w"]),
                  whole(w["w_a_out"]), whole(w["w_b_out"]), whole(w["w_o"])],
        out_specs=(row_tile(rows, d), row_tile(pa_rows, d)),
        out_shape=(jax.ShapeDtypeStruct(xb.shape, F32), jax.ShapeDtypeStruct(cab.shape, F32)),
        scratch_shapes=[pltpu.VMEM((rows, d), F32)],
        compiler_params=params,
    )(xb, cab, o_blk, w["norm_pre"], w["norm_post"], wm, wm, wm, wm, wm, w["w_gates"], w["w_gates"],
      w["b_gate"], w["conv_a_w"], w["gnorm_w"], w["w_a_out"], w["w_b_out"], w["w_o"])

    return (_from_blocks(y, t_len), _from_blocks(ca_new, CONV_A_WIDTH - 1),
            _from_blocks(cq_new, CONV_QKV_WIDTH - 1), s_new)


def _tail_rows(state, width):
    n, r, c = state.shape
    return jnp.concatenate([jnp.zeros((n, SUBLANES - r, c), state.dtype), state], axis=1)


def _lane_row(vec, offset):
    return jnp.zeros((1, HEAD), F32).at[0, offset:offset + vec.shape[0]].set(vec.astype(F32))


def kernel(x_prompt, x_sample, state_conv_a, state_conv_qkv, state_delta, meta, norm_pre, norm_post, w_in, b_gate,
           conv_a_w, conv_qkv_w, a_log, dt_bias, gnorm_w, w_a_out, w_b_out, w_o):
    depth = w_in.shape[0]
    assert depth == 1, "single-layer trunk"
    d = x_prompt.shape[-1]
    n_heads = d // HEAD
    bsz = x_prompt.shape[0]
    l = 0
    w = w_in[l]
    o_small = 8 * d
    w16 = w.astype(BF16)
    w_gates = w16[:, o_small + 2 * n_heads:]
    w_ab = jnp.concatenate([w16[:, o_small:o_small + 2 * n_heads],
                            jnp.zeros((d, HEAD - 2 * n_heads), BF16)], axis=1)
    weights = (
        norm_pre[l][None, :], norm_post[l][None, :], w16, w_gates, w_ab, b_gate[l][None, :],
        conv_a_w[l], conv_qkv_w[l], _lane_row(a_log[l], n_heads), _lane_row(dt_bias[l], n_heads),
        gnorm_w[l][None, :], w_a_out[l].astype(BF16), w_b_out[l].astype(BF16), w_o[l].astype(BF16),
    )

    zeros_a = jnp.zeros((1, SUBLANES, d), F32)
    zeros_q = jnp.zeros((1, SUBLANES, 3 * d), F32)
    zeros_s = jnp.zeros((1, n_heads, HEAD, HEAD), F32)
    _, ca_m, cq_m, s_m = _seq_layer(meta[None].astype(F32), zeros_a, zeros_q, zeros_s, weights,
                                    tile=N_META, chunk=N_META, shared_state=True)
    tile = min(256, x_prompt.shape[1])
    y_p, ca_p, cq_p, s_p = _seq_layer(x_prompt, ca_m, cq_m, s_m, weights,
                                      tile=tile, chunk=CHUNK, shared_state=True)
    def small_w(cols):
        return jnp.concatenate([cols, jnp.zeros((d, HEAD - n_heads), BF16)], axis=1)

    sample_w = dict(
        norm_pre=weights[0], norm_post=weights[1], w_main=w16, w_gates=w_gates, b_gate=weights[5],
        w_beta=small_w(w16[:, o_small:o_small + n_heads]),
        w_alpha=small_w(w16[:, o_small + n_heads:o_small + 2 * n_heads]),
        conv_a_w=conv_a_w[l], conv_qkv_w=conv_qkv_w[l], a_log0=_lane_row(a_log[l], 0),
        dt_bias0=_lane_row(dt_bias[l], 0), gnorm_w=weights[10], w_a_out=weights[11], w_b_out=weights[12],
        w_o=weights[13],
    )
    y_s, ca_s, cq_s, s_s = _sample_layer(x_sample, state_conv_a[l], state_conv_qkv[l], state_delta[l], sample_w)

    def tails(c, width):
        return c[None, :, SUBLANES - (width - 1):, :]

    return (y_p, y_s, tails(ca_p, CONV_A_WIDTH), tails(cq_p, CONV_QKV_WIDTH), s_p[None],
            ca_s[None], cq_s[None], s_s[None])
```

```python
import functools
import math

import jax
import jax.numpy as jnp
from jax import lax
from jax.experimental import pallas as pl
from jax.experimental.pallas import tpu as pltpu

HEAD = 128
CONV_A_WIDTH = 3
CONV_QKV_WIDTH = 4
CHUNK = 64
N_META = 16
EPS = 1e-6
SUBLANES = 8
BF16 = jnp.bfloat16
F32 = jnp.float32

V7X_VMEM_BYTES = 64 * 1024 * 1024
VMEM_LIMIT_BYTES = V7X_VMEM_BYTES * 7 // 8
PROMPT_TILE = 256
SAMPLE_TILE_BLOCKS = 8
SAMPLE_STATE_BLOCKS = 2


def _dot(a, b):
    return jnp.dot(a.astype(BF16), b.astype(BF16), preferred_element_type=F32)


def _dot_nt(a, b):
    return lax.dot_general(a.astype(BF16), b.astype(BF16), (((1,), (1,)), ((), ())),
                           preferred_element_type=F32)


def _dot_tn(a, b):
    return lax.dot_general(a.astype(BF16), b.astype(BF16), (((0,), (0,)), ((), ())),
                           preferred_element_type=F32)


def _sigmoid(x):
    return 1.0 / (1.0 + jnp.exp(-x))


def _silu(x):
    return x * _sigmoid(x)


def _softplus(x):
    return jnp.maximum(x, 0.0) + jnp.log1p(jnp.exp(-jnp.abs(x)))


def _rms(x, w):
    return x * lax.rsqrt(jnp.mean(x * x, axis=-1, keepdims=True) + EPS) * w


def _chunk_cumsum(x, chunk):
    row = lax.broadcasted_iota(jnp.int32, x.shape, 0) & (chunk - 1)
    sh = 1
    while sh < chunk:
        x = x + jnp.where(row >= sh, pltpu.roll(x, sh, 0), 0.0)
        sh *= 2
    return x


def _pad_lanes(piece, before, after):
    parts = [piece]
    if before:
        parts.insert(0, jnp.zeros((piece.shape[0], before), piece.dtype))
    if after:
        parts.append(jnp.zeros((piece.shape[0], after), piece.dtype))
    return jnp.concatenate(parts, axis=1) if len(parts) > 1 else piece


def _block_diag(cat):
    chunk, tt = cat.shape
    n = tt // chunk
    cat = cat.astype(BF16)
    if n == 1:
        return cat
    width = min(tt, HEAD)
    per_width = width // chunk
    lane_piece = lax.broadcasted_iota(jnp.int32, (chunk, width), 1) >> int(math.log2(chunk))
    rows = []
    for c in range(n):
        v, p = divmod(c, per_width)
        tile = cat[:, v * width:(v + 1) * width]
        if per_width > 1:
            tile = jnp.where(lane_piece == p, tile, jnp.zeros((), BF16))
        rows.append(_pad_lanes(tile, v * width, tt - (v + 1) * width))
    return jnp.concatenate(rows, axis=0)


def _delta_context(g_all, tt, chunk):
    ci = lax.broadcasted_iota(jnp.int32, (chunk, tt), 0)
    cj = lax.broadcasted_iota(jnp.int32, (chunk, tt), 1) & (chunk - 1)
    g_cum_all = _chunk_cumsum(g_all, chunk)
    return dict(incl_cat=ci >= cj, strict_cat=ci > cj,
                eye_cat=jnp.where(ci == cj, 1.0, 0.0).astype(F32),
                g_cum_all=g_cum_all, g_cum_t=g_cum_all.T, e_g_all=jnp.exp(g_cum_all))


def _column_per_chunk(col, chunk):
    tt = col.shape[0]
    width = min(tt, HEAD)
    per_width = width // chunk
    full = jnp.broadcast_to(col, (tt, width))
    lane_piece = lax.broadcasted_iota(jnp.int32, (chunk, width), 1) >> int(math.log2(chunk))
    parts = []
    for v in range(tt // width):
        acc = full[v * per_width * chunk:(v * per_width + 1) * chunk]
        for p in range(1, per_width):
            c = v * per_width + p
            acc = jnp.where(lane_piece == p, full[c * chunk:(c + 1) * chunk], acc)
        parts.append(acc)
    return jnp.concatenate(parts, axis=1)


def _chunks_side_by_side(a, chunk):
    return jnp.concatenate([a[c * chunk:(c + 1) * chunk] for c in range(a.shape[0] // chunk)], axis=1)


def _delta_tile(ctx, qkv_scr, beta_all, s_ref, o_scr, gnw, heads, *, chunk):
    tt = qkv_scr.shape[1]
    d = qkv_scr.shape[0] * HEAD // 3
    n_heads = d // HEAD
    n_c = tt // chunk
    shift = int(math.log2(chunk))
    eye_cat = ctx["eye_cat"]
    incl, strict = ctx["incl_cat"], ctx["strict_cat"]
    g_cum_all, g_cum_t, e_g_all = ctx["g_cum_all"], ctx["g_cum_t"], ctx["e_g_all"]

    qs, ks, rhs, gcs, attns, ms, ps = [], [], [], [], [], [], []
    for i, h in enumerate(heads):
        q = qkv_scr[h]
        k = qkv_scr[n_heads + h]
        v = qkv_scr[2 * n_heads + h]
        q = q * (lax.rsqrt(jnp.sum(q * q, axis=-1, keepdims=True) + EPS) * (HEAD ** -0.5))
        k = k * lax.rsqrt(jnp.sum(k * k, axis=-1, keepdims=True) + EPS)
        beta = beta_all[:, h:h + 1]
        g_cum = g_cum_all[:, n_heads + h:n_heads + h + 1]
        e_g = e_g_all[:, n_heads + h:n_heads + h + 1]
        diff = _column_per_chunk(g_cum, chunk) - g_cum_t[n_heads + h:n_heads + h + 1, :]
        decay = jnp.where(incl, jnp.exp(jnp.where(incl, diff, 0.0)), 0.0)
        kb = k * beta
        k16 = k.astype(BF16)
        k_bd = jnp.concatenate([_pad_lanes(k16[c * chunk:(c + 1) * chunk], c * HEAD, (n_c - 1 - c) * HEAD)
                                for c in range(n_c)], axis=0)
        kq = _dot_nt(jnp.concatenate([_chunks_side_by_side(kb, chunk), _chunks_side_by_side(q, chunk)], axis=0),
                     k_bd)
        m = -jnp.where(strict, kq[:chunk] * decay, 0.0)
        attns.append(_block_diag(kq[chunk:] * decay))
        ms.append(m)
        ps.append(eye_cat + m)
        qs.append(q * e_g)
        ks.append(k)
        rhs.append(jnp.concatenate([v * beta, kb * e_g], axis=1).astype(BF16))
        gcs.append(g_cum)
        if i % 2 == 1:
            yield

    ms = [_dot(m, _block_diag(m)) for m in ms]
    yield
    for k in range(1, shift):
        last = k == shift - 1
        for h in range(len(heads)):
            m_bd = _block_diag(ms[h])
            if last:
                ps[h] = ps[h] + _dot(ps[h], m_bd)
            else:
                prod = _dot(jnp.concatenate([ps[h], ms[h]], axis=0), m_bd)
                ps[h] = ps[h] + prod[:chunk]
                ms[h] = prod[chunk:]
        yield
    vks = [_dot(_block_diag(p), r) for p, r in zip(ps, rhs)]
    yield

    q_effs, o_locs, wbs, last_decays = [], [], [], []
    for h in range(len(heads)):
        av = _dot(attns[h], vks[h])
        o_locs.append(av[:, :HEAD])
        q_effs.append(qs[h] - av[:, HEAD:])
        wb_h, dl_h = [], []
        for c in range(n_c):
            rows = slice(c * chunk, (c + 1) * chunk)
            g_cum = gcs[h][rows]
            g_last = g_cum[chunk - 1:chunk, :]
            k_dec = ks[h][rows] * jnp.exp(g_last - g_cum)
            wb_h.append(_dot_tn(k_dec, jnp.concatenate([vks[h][rows, HEAD:], vks[h][rows, :HEAD]], axis=1)))
            dl_h.append(jnp.exp(g_last))
        wbs.append(wb_h)
        last_decays.append(dl_h)
        if h % 2 == 1:
            yield

    o_parts = [[] for _ in heads]
    for c in range(n_c):
        rows = slice(c * chunk, (c + 1) * chunk)
        for i, h in enumerate(heads):
            s = s_ref[0, h]
            prod = _dot(jnp.concatenate([wbs[i][c][:, :HEAD], q_effs[i][rows]], axis=0), s)
            s_ref[0, h] = s * last_decays[i][c] - prod[:HEAD] + wbs[i][c][:, HEAD:]
            o_parts[i].append(prod[HEAD:] + o_locs[i][rows])
        yield
    for i, h in enumerate(heads):
        o_scr[h] = _rms(jnp.concatenate(o_parts[i], axis=0), gnw)


def _to_lane_tiles(buf, row0, value, tile0=0):
    for j in range(value.shape[1] // HEAD):
        buf[tile0 + j, row0:row0 + value.shape[0], :] = value[:, j * HEAD:(j + 1) * HEAD]


def _from_lane_tiles(buf, row0, n_rows):
    return jnp.concatenate([buf[j, row0:row0 + n_rows, :] for j in range(buf.shape[0])], axis=1)


def _causal_conv(buf, w_ref, j):
    width = w_ref.shape[0]
    tt = buf.shape[1] - SUBLANES
    lanes = slice(j * HEAD, (j + 1) * HEAD)
    out = buf[j, SUBLANES:SUBLANES + tt, :] * w_ref[width - 1:width, lanes]
    for t in range(width - 1):
        off = SUBLANES - (width - 1) + t
        out = out + buf[j, off:off + tt, :] * w_ref[t:t + 1, lanes]
    return out


_EXHAUSTED = object()


def _alternate(first, second, lead):
    for _ in range(lead):
        next(first, None)
        yield
    live = [first, second]
    while live:
        still = []
        for g in live:
            if next(g, _EXHAUSTED) is not _EXHAUSTED:
                still.append(g)
                yield
        live = still


def _seq_kernel(x_ref, ca0_ref, cq0_ref, s0_ref, npre_ref, npost_ref, wmain_ref, wgate_ref, wab_ref, bgate_ref,
                caw_ref, cqw_ref, alog_ref, dtb_ref, gnw_ref, wa_ref, wb_ref, wo_ref,
                y_ref, ca_ref, cq_ref, s_ref,
                ubuf, qbuf, qkv_scr, o_scr, *, chunk):
    tt, d = x_ref.shape[1], x_ref.shape[2]
    t_idx = pl.program_id(1)

    @pl.when(t_idx == 0)
    def _():
        _to_lane_tiles(ubuf, 0, ca0_ref[0])
        _to_lane_tiles(qbuf, 0, cq0_ref[0])
        s_ref[0] = s0_ref[0]

    @pl.when(t_idx > 0)
    def _():
        ubuf[:, 0:SUBLANES, :] = ubuf[:, tt:tt + SUBLANES, :]
        qbuf[:, 0:SUBLANES, :] = qbuf[:, tt:tt + SUBLANES, :]

    x = x_ref[0]
    xn = _rms(x, npre_ref[...]).astype(BF16)

    def proj(k):
        return jnp.dot(xn, wmain_ref[:, k * d:(k + 1) * d], preferred_element_type=F32)

    def branch_b():
        for k in range(3):
            _to_lane_tiles(qbuf, SUBLANES, proj(4 + k), tile0=k * (d // HEAD))
            yield
        for j in range(3 * d // HEAD):
            qkv_scr[j] = _silu(_causal_conv(qbuf, cqw_ref, j))
        cq_ref[0] = _from_lane_tiles(qbuf, tt, SUBLANES)
        ab = jnp.dot(xn, wab_ref[...], preferred_element_type=F32)
        beta_all = _sigmoid(ab)
        g_all = -jnp.exp(alog_ref[...]) * _softplus(ab + dtb_ref[...])
        yield
        ctx = _delta_context(g_all, tt, chunk)
        n_heads = d // HEAD
        groups = [_delta_tile(ctx, qkv_scr, beta_all, s_ref, o_scr, gnw_ref[...], heads, chunk=chunk)
                  for heads in (range(0, n_heads // 2), range(n_heads // 2, n_heads))]
        yield from _alternate(groups[0], groups[1], lead=2)

    side = {}

    def branch_a():
        h_a = proj(0)
        yield
        _to_lane_tiles(ubuf, SUBLANES, proj(2) * h_a)
        yield
        conv_u = jnp.concatenate([_causal_conv(ubuf, caw_ref, j) for j in range(d // HEAD)], axis=1)
        ca_ref[0] = _from_lane_tiles(ubuf, tt, SUBLANES)
        y_a = proj(1) * conv_u
        yield
        y_a = y_a * _silu(proj(3))
        yield
        gate_a = _sigmoid(jnp.dot(xn, wgate_ref[:, 0:d], preferred_element_type=F32) + bgate_ref[:, 0:d])
        yield
        side["merged"] = gate_a * _dot(y_a, wa_ref[...])
        yield
        side["silu_z_b"] = _silu(proj(7))
        yield
        side["gate_b"] = _sigmoid(jnp.dot(xn, wgate_ref[:, d:2 * d], preferred_element_type=F32)
                                  + bgate_ref[:, d:2 * d])

    for _ in _alternate(branch_b(), branch_a(), lead=2):
        pass

    y_b = _from_lane_tiles(o_scr, 0, tt) * side["silu_z_b"]
    merged = side["merged"] + side["gate_b"] * _dot(y_b, wb_ref[...])
    y_ref[0] = x + _rms(_dot(merged, wo_ref[...]), npost_ref[...])


def _const_spec(shape):
    nd = len(shape)
    return pl.BlockSpec(shape, lambda b, t: (0,) * nd)


def _seq_layer(x, ca0, cq0, s0, weights, *, tile, chunk):
    n, t_len, d = x.shape
    n_heads = d // HEAD
    in_specs = [
        pl.BlockSpec((1, tile, d), lambda b, t: (b, t, 0)),
        _const_spec((1, SUBLANES, d)),
        _const_spec((1, SUBLANES, 3 * d)),
        _const_spec((1, n_heads, HEAD, HEAD)),
    ] + [_const_spec(w.shape) for w in weights]
    in_specs[4 + 2] = _const_spec((d, 8 * d))
    out_shape = (
        jax.ShapeDtypeStruct((n, t_len, d), F32),
        jax.ShapeDtypeStruct((n, SUBLANES, d), F32),
        jax.ShapeDtypeStruct((n, SUBLANES, 3 * d), F32),
        jax.ShapeDtypeStruct((n, n_heads, HEAD, HEAD), F32),
    )
    out_specs = (
        pl.BlockSpec((1, tile, d), lambda b, t: (b, t, 0)),
        pl.BlockSpec((1, SUBLANES, d), lambda b, t: (b, 0, 0)),
        pl.BlockSpec((1, SUBLANES, 3 * d), lambda b, t: (b, 0, 0)),
        pl.BlockSpec((1, n_heads, HEAD, HEAD), lambda b, t: (b, 0, 0, 0)),
    )
    scratch = [
        pltpu.VMEM((d // HEAD, SUBLANES + tile, HEAD), F32),
        pltpu.VMEM((3 * d // HEAD, SUBLANES + tile, HEAD), F32),
        pltpu.VMEM((3 * d // HEAD, tile, HEAD), F32),
        pltpu.VMEM((d // HEAD, tile, HEAD), F32),
    ]
    return pl.pallas_call(
        functools.partial(_seq_kernel, chunk=chunk),
        grid=(n, t_len // tile),
        in_specs=in_specs, out_specs=out_specs, out_shape=out_shape, scratch_shapes=scratch,
        compiler_params=pltpu.CompilerParams(dimension_semantics=("arbitrary", "arbitrary"),
                                             vmem_limit_bytes=VMEM_LIMIT_BYTES),
    )(x, ca0, cq0, s0, *weights)


SEQ_BLOCK = SUBLANES
PAIR_ROWS = 4 * SUBLANES


def _blocked_conv(prefix_ref, new, w_ref, out_ref, tail_ref, *, width, t_len, act):
    pw = (width - 1) * SEQ_BLOCK
    br = t_len * SEQ_BLOCK
    for j in range(new.shape[0] // br):
        ext = jnp.concatenate([prefix_ref[j * pw:(j + 1) * pw, :], new[j * br:(j + 1) * br, :]], axis=0)
        conv = ext[pw:pw + br] * w_ref[width - 1:width, :]
        for tap in range(width - 1):
            conv = conv + ext[tap * SEQ_BLOCK:tap * SEQ_BLOCK + br] * w_ref[tap:tap + 1, :]
        out_ref[j * br:(j + 1) * br, :] = act(conv)
        tail_ref[j * pw:(j + 1) * pw, :] = ext[br:br + pw]


def _time_slab(a, t, t_len, lanes=slice(None)):
    n_blocks = a.shape[0] // (t_len * SEQ_BLOCK)
    return jnp.concatenate([a[(j * t_len + t) * SEQ_BLOCK:(j * t_len + t + 1) * SEQ_BLOCK, lanes]
                            for j in range(n_blocks)], axis=0)


def _sample_pre_kernel(x_ref, cq0_ref, npre_ref, wq_ref, wk_ref, wv_ref, wbeta_ref, walpha_ref, cqw_ref,
                       alog_ref, dtb_ref, cq_ref, pre_ref, qkv_scr, *, t_len):
    rows, d = x_ref.shape
    n_heads = d // HEAD
    n_blocks = rows // (t_len * SEQ_BLOCK)
    n_seq = n_blocks * SEQ_BLOCK
    assert 2 * t_len == SUBLANES
    xn = _rms(x_ref[...], npre_ref[...]).astype(BF16)
    qkv_pre = jnp.concatenate([jnp.dot(xn, w[...], preferred_element_type=F32) for w in (wq_ref, wk_ref, wv_ref)],
                              axis=1)
    _blocked_conv(cq0_ref, qkv_pre, cqw_ref, qkv_scr, cq_ref, width=CONV_QKV_WIDTH, t_len=t_len, act=_silu)

    beta_full = _sigmoid(jnp.dot(xn, wbeta_ref[...], preferred_element_type=F32))
    g_full = -jnp.exp(alog_ref[...]) * _softplus(jnp.dot(xn, walpha_ref[...], preferred_element_type=F32)
                                                 + dtb_ref[...])
    beta = [_time_slab(beta_full, t, t_len) for t in range(t_len)]
    g_cum = []
    for t in range(t_len):
        g_t = _time_slab(g_full, t, t_len)
        g_cum.append(g_t if t == 0 else g_cum[-1] + g_t)
    e_g = [jnp.exp(g) for g in g_cum]
    decay = {(i, j): jnp.exp(g_cum[i] - g_cum[j]) for i in range(t_len) for j in range(i + 1)}
    k_decay = [jnp.exp(g_cum[-1] - g) for g in g_cum]
    zeros = jnp.zeros((n_seq, HEAD), F32)

    for h in range(n_heads):
        def col(a):
            return a[:, h:h + 1]
        q = [_time_slab(qkv_scr, t, t_len, slice(h * HEAD, (h + 1) * HEAD)) for t in range(t_len)]
        k = [_time_slab(qkv_scr, t, t_len, slice(d + h * HEAD, d + (h + 1) * HEAD)) for t in range(t_len)]
        v = [_time_slab(qkv_scr, t, t_len, slice(2 * d + h * HEAD, 2 * d + (h + 1) * HEAD)) for t in range(t_len)]
        q = [a * (lax.rsqrt(jnp.sum(a * a, axis=-1, keepdims=True) + EPS) * (HEAD ** -0.5)) for a in q]
        k = [a * lax.rsqrt(jnp.sum(a * a, axis=-1, keepdims=True) + EPS) for a in k]
        kb = [k[t] * col(beta[t]) for t in range(t_len)]
        a_low = {(i, j): jnp.sum(kb[i] * k[j], axis=-1, keepdims=True) * col(decay[i, j])
                 for i in range(t_len) for j in range(i)}
        attn = {(i, j): jnp.sum(q[i] * k[j], axis=-1, keepdims=True) * col(decay[i, j])
                for i in range(t_len) for j in range(i + 1)}
        t_inv = {}
        for i in range(t_len):
            for j in range(i):
                acc = a_low[i, j]
                for m in range(j + 1, i):
                    acc = acc + a_low[i, m] * t_inv[m, j]
                t_inv[i, j] = -acc
        vb = [v[t] * col(beta[t]) for t in range(t_len)]
        kbe = [kb[t] * col(e_g[t]) for t in range(t_len)]
        value, k_cum = [], []
        for i in range(t_len):
            val, kc = vb[i], kbe[i]
            for j in range(i):
                val = val + t_inv[i, j] * vb[j]
                kc = kc + t_inv[i, j] * kbe[j]
            value.append(val)
            k_cum.append(kc)
        q_eff, o_loc = [], []
        for i in range(t_len):
            qe = q[i] * col(e_g[i])
            ol = None
            for j in range(i + 1):
                qe = qe - attn[i, j] * k_cum[j]
                ol = attn[i, j] * value[j] if ol is None else ol + attn[i, j] * value[j]
            q_eff.append(qe)
            o_loc.append(ol)
        k_dec = [k[t] * col(k_decay[t]) for t in range(t_len)]
        last_decay = jnp.broadcast_to(col(e_g[-1]), (n_seq, HEAD))
        slabs = (q_eff + k_cum + o_loc + value + [zeros] * t_len + k_dec
                 + [last_decay] + [zeros] * (SUBLANES - 1))
        for group, slab in enumerate(slabs):
            for j in range(n_blocks):
                base = j * PAIR_ROWS * SEQ_BLOCK + group * SEQ_BLOCK
                pre_ref[h, base:base + SEQ_BLOCK, :] = slab[j * SEQ_BLOCK:(j + 1) * SEQ_BLOCK]


def _sample_state_kernel(pre_ref, s_ref, o_ref, s_out_ref):
    n_heads = pre_ref.shape[0]
    group = SUBLANES * SEQ_BLOCK
    row = lax.broadcasted_iota(jnp.int32, (SUBLANES, HEAD), 0)
    sign = jnp.where(row < SUBLANES // 2, 1.0, -1.0).astype(F32)

    def seq_rows(first):
        return pl.ds(first, SUBLANES, stride=SEQ_BLOCK)

    for blk in range(s_ref.shape[0] // SEQ_BLOCK):
        pre0 = blk * PAIR_ROWS * SEQ_BLOCK
        out0 = blk * group
        for i in range(SEQ_BLOCK):
            seq = blk * SEQ_BLOCK + i
            prods = [_dot(pre_ref.at[h][seq_rows(pre0 + i), :], s_ref[seq, h]) for h in range(n_heads)]
            for h in range(n_heads):
                o_vnew = pre_ref.at[h][seq_rows(pre0 + group + i), :] + prods[h] * sign
                k_dec = pre_ref.at[h][seq_rows(pre0 + 2 * group + i), :]
                last_decay = pre_ref[h, pre0 + 3 * group + i:pre0 + 3 * group + i + 1, :]
                o_ref.at[h][seq_rows(out0 + i), :] = o_vnew
                s_out_ref[seq, h] = s_ref[seq, h] * last_decay + _dot_tn(k_dec, o_vnew)


def _sample_post_kernel(x_ref, ca0_ref, o_ref, npre_ref, npost_ref, wh_ref, wb_ref, wc_ref, wz_ref, wzb_ref,
                        wga_ref, wgb_ref, bgate_ref, caw_ref, gnw_ref, wa_ref, wbo_ref, wo_ref,
                        y_ref, ca_ref, conv_scr, *, t_len):
    rows, d = x_ref.shape
    n_heads = d // HEAD
    br = t_len * SEQ_BLOCK
    n_blocks = rows // br
    x = x_ref[...]
    xn = _rms(x, npre_ref[...]).astype(BF16)

    def proj(w_ref):
        return jnp.dot(xn, w_ref[...], preferred_element_type=F32)

    _blocked_conv(ca0_ref, proj(wc_ref) * proj(wh_ref), caw_ref, conv_scr, ca_ref, width=CONV_A_WIDTH, t_len=t_len,
                  act=lambda a: a)
    y_a = proj(wb_ref) * conv_scr[...] * _silu(proj(wz_ref))
    merged = _sigmoid(proj(wga_ref) + bgate_ref[:, 0:d]) * _dot(y_a, wa_ref[...])
    group = SUBLANES * SEQ_BLOCK
    o_norm = [_rms(jnp.concatenate([o_ref[h, j * group:j * group + br, :] for j in range(n_blocks)], axis=0),
                   gnw_ref[...]) for h in range(n_heads)]
    y_b = jnp.concatenate(o_norm, axis=1) * _silu(proj(wzb_ref))
    merged = merged + _sigmoid(proj(wgb_ref) + bgate_ref[:, d:2 * d]) * _dot(y_b, wbo_ref[...])
    y_ref[...] = x + _rms(_dot(merged, wo_ref[...]), npost_ref[...])


def _to_blocks(a):
    n, t, c = a.shape
    return a.reshape(n // SEQ_BLOCK, SEQ_BLOCK, t, c).transpose(0, 2, 1, 3).reshape(n * t, c)


def _from_blocks(a, t):
    n = a.shape[0] // t
    return a.reshape(n // SEQ_BLOCK, t, SEQ_BLOCK, a.shape[-1]).transpose(0, 2, 1, 3).reshape(n, t, a.shape[-1])


def _sample_layer(x, ca0, cq0, s0, w):
    n, t_len, d = x.shape
    n_heads = d // HEAD
    n_blocks = n // SEQ_BLOCK
    tile_blocks = min(SAMPLE_TILE_BLOCKS, n_blocks)
    n_tiles = n_blocks // tile_blocks
    rows = tile_blocks * t_len * SEQ_BLOCK
    xb, cab, cqb = _to_blocks(x), _to_blocks(ca0), _to_blocks(cq0)
    wm = w["w_main"]

    def col_block(k):
        return pl.BlockSpec((d, d), lambda t: (0, k))

    def whole(a):
        nd = a.ndim
        return pl.BlockSpec(a.shape, lambda t: (0,) * nd)

    def row_tile(r, c):
        return pl.BlockSpec((r, c), lambda t: (t, 0))

    params = pltpu.CompilerParams(dimension_semantics=("arbitrary",), vmem_limit_bytes=VMEM_LIMIT_BYTES)
    pa_rows = (CONV_A_WIDTH - 1) * SEQ_BLOCK * tile_blocks
    pq_rows = (CONV_QKV_WIDTH - 1) * SEQ_BLOCK * tile_blocks
    pre_rows = PAIR_ROWS * SEQ_BLOCK

    cq_new, pre = pl.pallas_call(
        functools.partial(_sample_pre_kernel, t_len=t_len),
        grid=(n_tiles,),
        in_specs=[row_tile(rows, d), row_tile(pq_rows, 3 * d), whole(w["norm_pre"]),
                  col_block(4), col_block(5), col_block(6), whole(w["w_beta"]), whole(w["w_alpha"]),
                  whole(w["conv_qkv_w"]), whole(w["a_log0"]), whole(w["dt_bias0"])],
        out_specs=(row_tile(pq_rows, 3 * d),
                   pl.BlockSpec((n_heads, tile_blocks * pre_rows, HEAD), lambda t: (0, t, 0))),
        out_shape=(jax.ShapeDtypeStruct(cqb.shape, F32),
                   jax.ShapeDtypeStruct((n_heads, n_blocks * pre_rows, HEAD), F32)),
        scratch_shapes=[pltpu.VMEM((rows, 3 * d), F32)],
        compiler_params=params,
    )(xb, cqb, w["norm_pre"], wm, wm, wm, w["w_beta"], w["w_alpha"], w["conv_qkv_w"], w["a_log0"], w["dt_bias0"])

    o_rows = SUBLANES * SEQ_BLOCK
    step_blocks = SAMPLE_STATE_BLOCKS if n_blocks % SAMPLE_STATE_BLOCKS == 0 else 1
    step_seqs = step_blocks * SEQ_BLOCK
    o_blk, s_new = pl.pallas_call(
        _sample_state_kernel,
        grid=(n_blocks // step_blocks,),
        in_specs=[pl.BlockSpec((n_heads, step_blocks * pre_rows, HEAD), lambda j: (0, j, 0)),
                  pl.BlockSpec((step_seqs, n_heads, HEAD, HEAD), lambda j: (j, 0, 0, 0))],
        out_specs=(pl.BlockSpec((n_heads, step_blocks * o_rows, HEAD), lambda j: (0, j, 0)),
                   pl.BlockSpec((step_seqs, n_heads, HEAD, HEAD), lambda j: (j, 0, 0, 0))),
        out_shape=(jax.ShapeDtypeStruct((n_heads, n_blocks * o_rows, HEAD), F32),
                   jax.ShapeDtypeStruct(s0.shape, F32)),
        compiler_params=params,
    )(pre, s0)

    y, ca_new = pl.pallas_call(
        functools.partial(_sample_post_kernel, t_len=t_len),
        grid=(n_tiles,),
        in_specs=[row_tile(rows, d), row_tile(pa_rows, d),
                  pl.BlockSpec((n_heads, tile_blocks * o_rows, HEAD), lambda t: (0, t, 0)),
                  whole(w["norm_pre"]), whole(w["norm_post"]),
                  col_block(0), col_block(1), col_block(2), col_block(3), col_block(7), col_block(0), col_block(1),
                  whole(w["b_gate"]), whole(w["conv_a_w"]), whole(w["gnorm_w"]),
                  whole(w["w_a_out"]), whole(w["w_b_out"]), whole(w["w_o"])],
        out_specs=(row_tile(rows, d), row_tile(pa_rows, d)),
        out_shape=(jax.ShapeDtypeStruct(xb.shape, F32), jax.ShapeDtypeStruct(cab.shape, F32)),
        scratch_shapes=[pltpu.VMEM((rows, d), F32)],
        compiler_params=params,
    )(xb, cab, o_blk, w["norm_pre"], w["norm_post"], wm, wm, wm, wm, wm, w["w_gates"], w["w_gates"],
      w["b_gate"], w["conv_a_w"], w["gnorm_w"], w["w_a_out"], w["w_b_out"], w["w_o"])

    return (_from_blocks(y, t_len), _from_blocks(ca_new, CONV_A_WIDTH - 1),
            _from_blocks(cq_new, CONV_QKV_WIDTH - 1), s_new)


def _lane_row(vec, offset):
    return jnp.zeros((1, HEAD), F32).at[0, offset:offset + vec.shape[0]].set(vec.astype(F32))


def kernel(x_prompt, x_sample, state_conv_a, state_conv_qkv, state_delta, meta, norm_pre, norm_post, w_in, b_gate,
           conv_a_w, conv_qkv_w, a_log, dt_bias, gnorm_w, w_a_out, w_b_out, w_o):
    depth = w_in.shape[0]
    assert depth == 1, "single-layer trunk"
    d = x_prompt.shape[-1]
    n_heads = d // HEAD
    l = 0
    w = w_in[l]
    o_small = 8 * d
    w16 = w.astype(BF16)
    w_gates = w16[:, o_small + 2 * n_heads:]
    w_ab = jnp.concatenate([w16[:, o_small:o_small + 2 * n_heads],
                            jnp.zeros((d, HEAD - 2 * n_heads), BF16)], axis=1)
    weights = (
        norm_pre[l][None, :], norm_post[l][None, :], w16, w_gates, w_ab, b_gate[l][None, :],
        conv_a_w[l], conv_qkv_w[l], _lane_row(a_log[l], n_heads), _lane_row(dt_bias[l], n_heads),
        gnorm_w[l][None, :], w_a_out[l].astype(BF16), w_b_out[l].astype(BF16), w_o[l].astype(BF16),
    )

    zeros_a = jnp.zeros((1, SUBLANES, d), F32)
    zeros_q = jnp.zeros((1, SUBLANES, 3 * d), F32)
    zeros_s = jnp.zeros((1, n_heads, HEAD, HEAD), F32)
    _, ca_m, cq_m, s_m = _seq_layer(meta[None].astype(F32), zeros_a, zeros_q, zeros_s, weights,
                                    tile=N_META, chunk=N_META)
    y_p, ca_p, cq_p, s_p = _seq_layer(x_prompt, ca_m, cq_m, s_m, weights,
                                      tile=min(PROMPT_TILE, x_prompt.shape[1]), chunk=CHUNK)

    def small_w(cols):
        return jnp.concatenate([cols, jnp.zeros((d, HEAD - n_heads), BF16)], axis=1)

    sample_w = dict(
        norm_pre=weights[0], norm_post=weights[1], w_main=w16, w_gates=w_gates, b_gate=weights[5],
        w_beta=small_w(w16[:, o_small:o_small + n_heads]),
        w_alpha=small_w(w16[:, o_small + n_heads:o_small + 2 * n_heads]),
        conv_a_w=conv_a_w[l], conv_qkv_w=conv_qkv_w[l], a_log0=_lane_row(a_log[l], 0),
        dt_bias0=_lane_row(dt_bias[l], 0), gnorm_w=weights[10], w_a_out=weights[11], w_b_out=weights[12],
        w_o=weights[13],
    )
    y_s, ca_s, cq_s, s_s = _sample_layer(x_sample, state_conv_a[l], state_conv_qkv[l], state_delta[l], sample_w)

    def tails(c, width):
        return c[None, :, SUBLANES - (width - 1):, :]

    return (y_p, y_s, tails(ca_p, CONV_A_WIDTH), tails(cq_p, CONV_QKV_WIDTH), s_p[None],
            ca_s[None], cq_s[None], s_s[None])
```

```python
import functools
import math

import jax
import jax.numpy as jnp
from jax import lax
from jax.experimental import pallas as pl
from jax.experimental.pallas import tpu as pltpu

HEAD = 128
CONV_A_WIDTH = 3
CONV_QKV_WIDTH = 4
CHUNK = 64
N_META = 16
EPS = 1e-6
SUBLANES = 8
BF16 = jnp.bfloat16
F32 = jnp.float32

V7X_VMEM_BYTES = 64 * 1024 * 1024
VMEM_LIMIT_BYTES = V7X_VMEM_BYTES * 7 // 8
PROMPT_TILE = 256
SAMPLE_TILE_BLOCKS = 8
SAMPLE_STATE_BLOCKS = 2


def _dot(a, b):
    return jnp.dot(a.astype(BF16), b.astype(BF16), preferred_element_type=F32)


def _dot_nt(a, b):
    return lax.dot_general(a.astype(BF16), b.astype(BF16), (((1,), (1,)), ((), ())),
                           preferred_element_type=F32)


def _dot_tn(a, b):
    return lax.dot_general(a.astype(BF16), b.astype(BF16), (((0,), (0,)), ((), ())),
                           preferred_element_type=F32)


def _sigmoid(x):
    return 1.0 / (1.0 + jnp.exp(-x))


def _silu(x):
    return x * _sigmoid(x)


def _softplus(x):
    return jnp.maximum(x, 0.0) + jnp.log1p(jnp.exp(-jnp.abs(x)))


def _rms(x, w):
    return x * lax.rsqrt(jnp.mean(x * x, axis=-1, keepdims=True) + EPS) * w


def _chunk_cumsum(x, chunk):
    row = lax.broadcasted_iota(jnp.int32, x.shape, 0) & (chunk - 1)
    sh = 1
    while sh < chunk:
        x = x + jnp.where(row >= sh, pltpu.roll(x, sh, 0), 0.0)
        sh *= 2
    return x


def _pad_lanes(piece, before, after):
    parts = [piece]
    if before:
        parts.insert(0, jnp.zeros((piece.shape[0], before), piece.dtype))
    if after:
        parts.append(jnp.zeros((piece.shape[0], after), piece.dtype))
    return jnp.concatenate(parts, axis=1) if len(parts) > 1 else piece


def _block_diag(cat):
    chunk, tt = cat.shape
    n = tt // chunk
    cat = cat.astype(BF16)
    if n == 1:
        return cat
    width = min(tt, HEAD)
    per_width = width // chunk
    lane_piece = lax.broadcasted_iota(jnp.int32, (chunk, width), 1) >> int(math.log2(chunk))
    rows = []
    for c in range(n):
        v, p = divmod(c, per_width)
        tile = cat[:, v * width:(v + 1) * width]
        if per_width > 1:
            tile = jnp.where(lane_piece == p, tile, jnp.zeros((), BF16))
        rows.append(_pad_lanes(tile, v * width, tt - (v + 1) * width))
    return jnp.concatenate(rows, axis=0)


def _delta_context(g_all, tt, chunk):
    ci = lax.broadcasted_iota(jnp.int32, (chunk, tt), 0)
    cj = lax.broadcasted_iota(jnp.int32, (chunk, tt), 1) & (chunk - 1)
    g_cum_all = _chunk_cumsum(g_all, chunk)
    return dict(incl_cat=ci >= cj, strict_cat=ci > cj,
                eye_cat=jnp.where(ci == cj, 1.0, 0.0).astype(F32),
                g_cum_all=g_cum_all, g_cum_t=g_cum_all.T, e_g_all=jnp.exp(g_cum_all))


def _column_per_chunk(col, chunk):
    tt = col.shape[0]
    width = min(tt, HEAD)
    per_width = width // chunk
    full = jnp.broadcast_to(col, (tt, width))
    lane_piece = lax.broadcasted_iota(jnp.int32, (chunk, width), 1) >> int(math.log2(chunk))
    parts = []
    for v in range(tt // width):
        acc = full[v * per_width * chunk:(v * per_width + 1) * chunk]
        for p in range(1, per_width):
            c = v * per_width + p
            acc = jnp.where(lane_piece == p, full[c * chunk:(c + 1) * chunk], acc)
        parts.append(acc)
    return jnp.concatenate(parts, axis=1)


def _chunks_side_by_side(a, chunk):
    return jnp.concatenate([a[c * chunk:(c + 1) * chunk] for c in range(a.shape[0] // chunk)], axis=1)


def _delta_tile(ctx, qkv_scr, beta_all, s_ref, o_scr, gnw, heads, *, chunk):
    tt = qkv_scr.shape[1]
    d = qkv_scr.shape[0] * HEAD // 3
    n_heads = d // HEAD
    n_c = tt // chunk
    shift = int(math.log2(chunk))
    eye_cat = ctx["eye_cat"]
    incl, strict = ctx["incl_cat"], ctx["strict_cat"]
    g_cum_all, g_cum_t, e_g_all = ctx["g_cum_all"], ctx["g_cum_t"], ctx["e_g_all"]

    qs, ks, rhs, gcs, attns, ms, ps = [], [], [], [], [], [], []
    for i, h in enumerate(heads):
        q = qkv_scr[h]
        k = qkv_scr[n_heads + h]
        v = qkv_scr[2 * n_heads + h]
        q = q * (lax.rsqrt(jnp.sum(q * q, axis=-1, keepdims=True) + EPS) * (HEAD ** -0.5))
        k = k * lax.rsqrt(jnp.sum(k * k, axis=-1, keepdims=True) + EPS)
        beta = beta_all[:, h:h + 1]
        g_cum = g_cum_all[:, n_heads + h:n_heads + h + 1]
        e_g = e_g_all[:, n_heads + h:n_heads + h + 1]
        diff = _column_per_chunk(g_cum, chunk) - g_cum_t[n_heads + h:n_heads + h + 1, :]
        decay = jnp.where(incl, jnp.exp(jnp.where(incl, diff, 0.0)), 0.0)
        kb = k * beta
        k16 = k.astype(BF16)
        k_bd = jnp.concatenate([_pad_lanes(k16[c * chunk:(c + 1) * chunk], c * HEAD, (n_c - 1 - c) * HEAD)
                                for c in range(n_c)], axis=0)
        kq = _dot_nt(jnp.concatenate([_chunks_side_by_side(kb, chunk), _chunks_side_by_side(q, chunk)], axis=0),
                     k_bd)
        m = -jnp.where(strict, kq[:chunk] * decay, 0.0)
        attns.append(_block_diag(kq[chunk:] * decay))
        ms.append(m)
        ps.append(eye_cat + m)
        qs.append(q * e_g)
        ks.append(k)
        rhs.append(jnp.concatenate([v * beta, kb * e_g], axis=1).astype(BF16))
        gcs.append(g_cum)
        yield

    ms = [_dot(m, _block_diag(m)) for m in ms]
    yield
    for k in range(1, shift):
        last = k == shift - 1
        for h in range(len(heads)):
            m_bd = _block_diag(ms[h])
            if last:
                ps[h] = ps[h] + _dot(ps[h], m_bd)
            else:
                prod = _dot(jnp.concatenate([ps[h], ms[h]], axis=0), m_bd)
                ps[h] = ps[h] + prod[:chunk]
                ms[h] = prod[chunk:]
        yield
    vks = [_dot(_block_diag(p), r) for p, r in zip(ps, rhs)]
    yield

    q_effs, o_locs, wbs, last_decays = [], [], [], []
    for h in range(len(heads)):
        av = _dot(attns[h], vks[h])
        o_locs.append(av[:, :HEAD])
        q_effs.append(qs[h] - av[:, HEAD:])
        wb_h, dl_h = [], []
        for c in range(n_c):
            rows = slice(c * chunk, (c + 1) * chunk)
            g_cum = gcs[h][rows]
            g_last = g_cum[chunk - 1:chunk, :]
            k_dec = ks[h][rows] * jnp.exp(g_last - g_cum)
            wb_h.append(_dot_tn(k_dec, jnp.concatenate([vks[h][rows, HEAD:], vks[h][rows, :HEAD]], axis=1)))
            dl_h.append(jnp.exp(g_last))
        wbs.append(wb_h)
        last_decays.append(dl_h)
        yield

    o_parts = [[] for _ in heads]
    for c in range(n_c):
        rows = slice(c * chunk, (c + 1) * chunk)
        for i, h in enumerate(heads):
            s = s_ref[0, h]
            prod = _dot(jnp.concatenate([wbs[i][c][:, :HEAD], q_effs[i][rows]], axis=0), s)
            s_ref[0, h] = s * last_decays[i][c] - prod[:HEAD] + wbs[i][c][:, HEAD:]
            o_parts[i].append(prod[HEAD:] + o_locs[i][rows])
        yield
    for i, h in enumerate(heads):
        o_scr[h] = _rms(jnp.concatenate(o_parts[i], axis=0), gnw)


def _to_lane_tiles(buf, row0, value, tile0=0):
    for j in range(value.shape[1] // HEAD):
        buf[tile0 + j, row0:row0 + value.shape[0], :] = value[:, j * HEAD:(j + 1) * HEAD]


def _from_lane_tiles(buf, row0, n_rows):
    return jnp.concatenate([buf[j, row0:row0 + n_rows, :] for j in range(buf.shape[0])], axis=1)


def _causal_conv(buf, w_ref, j):
    width = w_ref.shape[0]
    tt = buf.shape[1] - SUBLANES
    lanes = slice(j * HEAD, (j + 1) * HEAD)
    out = buf[j, SUBLANES:SUBLANES + tt, :] * w_ref[width - 1:width, lanes]
    for t in range(width - 1):
        off = SUBLANES - (width - 1) + t
        out = out + buf[j, off:off + tt, :] * w_ref[t:t + 1, lanes]
    return out


_EXHAUSTED = object()


def _alternate(first, second, lead):
    for _ in range(lead):
        next(first, None)
        yield
    live = [first, second]
    while live:
        still = []
        for g in live:
            if next(g, _EXHAUSTED) is not _EXHAUSTED:
                still.append(g)
                yield
        live = still


def _seq_kernel(x_ref, ca0_ref, cq0_ref, s0_ref, npre_ref, npost_ref, wmain_ref, wgate_ref, wab_ref, bgate_ref,
                caw_ref, cqw_ref, alog_ref, dtb_ref, gnw_ref, wa_ref, wb_ref, wo_ref,
                y_ref, ca_ref, cq_ref, s_ref,
                ubuf, qbuf, qkv_scr, o_scr, *, chunk):
    tt, d = x_ref.shape[1], x_ref.shape[2]
    t_idx = pl.program_id(1)

    @pl.when(t_idx == 0)
    def _():
        _to_lane_tiles(ubuf, 0, ca0_ref[0])
        _to_lane_tiles(qbuf, 0, cq0_ref[0])
        s_ref[0] = s0_ref[0]

    @pl.when(t_idx > 0)
    def _():
        ubuf[:, 0:SUBLANES, :] = ubuf[:, tt:tt + SUBLANES, :]
        qbuf[:, 0:SUBLANES, :] = qbuf[:, tt:tt + SUBLANES, :]

    xn = _rms(x_ref[0], npre_ref[...]).astype(BF16)

    def proj(k):
        return jnp.dot(xn, wmain_ref[:, k * d:(k + 1) * d], preferred_element_type=F32)

    def branch_b():
        for k in range(3):
            _to_lane_tiles(qbuf, SUBLANES, proj(4 + k), tile0=k * (d // HEAD))
            yield
        for j in range(3 * d // HEAD):
            qkv_scr[j] = _silu(_causal_conv(qbuf, cqw_ref, j))
        cq_ref[0] = _from_lane_tiles(qbuf, tt, SUBLANES)
        ab = jnp.dot(xn, wab_ref[...], preferred_element_type=F32)
        beta_all = _sigmoid(ab)
        g_all = -jnp.exp(alog_ref[...]) * _softplus(ab + dtb_ref[...])
        yield
        ctx = _delta_context(g_all, tt, chunk)
        n_heads = d // HEAD
        groups = [_delta_tile(ctx, qkv_scr, beta_all, s_ref, o_scr, gnw_ref[...], heads, chunk=chunk)
                  for heads in (range(0, n_heads // 2), range(n_heads // 2, n_heads))]
        yield from _alternate(groups[0], groups[1], lead=2)

    side = {}

    def branch_a():
        h_a = proj(0)
        yield
        _to_lane_tiles(ubuf, SUBLANES, proj(2) * h_a)
        yield
        conv_u = jnp.concatenate([_causal_conv(ubuf, caw_ref, j) for j in range(d // HEAD)], axis=1)
        ca_ref[0] = _from_lane_tiles(ubuf, tt, SUBLANES)
        y_a = proj(1) * conv_u
        yield
        y_a = y_a * _silu(proj(3))
        yield
        gate_a = _sigmoid(jnp.dot(xn, wgate_ref[:, 0:d], preferred_element_type=F32) + bgate_ref[:, 0:d])
        yield
        side["merged"] = gate_a * _dot(y_a, wa_ref[...])
        yield
        side["silu_z_b"] = _silu(proj(7))
        yield
        side["gate_b"] = _sigmoid(jnp.dot(xn, wgate_ref[:, d:2 * d], preferred_element_type=F32)
                                  + bgate_ref[:, d:2 * d])

    for _ in _alternate(branch_b(), branch_a(), lead=2):
        pass

    y_b = _from_lane_tiles(o_scr, 0, tt) * side["silu_z_b"]
    merged = side["merged"] + side["gate_b"] * _dot(y_b, wb_ref[...])
    y_ref[0] = x_ref[0] + _rms(_dot(merged, wo_ref[...]), npost_ref[...])


def _const_spec(shape):
    nd = len(shape)
    return pl.BlockSpec(shape, lambda b, t: (0,) * nd)


def _seq_layer(x, ca0, cq0, s0, weights, *, tile, chunk):
    n, t_len, d = x.shape
    n_heads = d // HEAD
    in_specs = [
        pl.BlockSpec((1, tile, d), lambda b, t: (b, t, 0)),
        _const_spec((1, SUBLANES, d)),
        _const_spec((1, SUBLANES, 3 * d)),
        _const_spec((1, n_heads, HEAD, HEAD)),
    ] + [_const_spec(w.shape) for w in weights]
    in_specs[4 + 2] = _const_spec((d, 8 * d))
    out_shape = (
        jax.ShapeDtypeStruct((n, t_len, d), F32),
        jax.ShapeDtypeStruct((n, SUBLANES, d), F32),
        jax.ShapeDtypeStruct((n, SUBLANES, 3 * d), F32),
        jax.ShapeDtypeStruct((n, n_heads, HEAD, HEAD), F32),
    )
    out_specs = (
        pl.BlockSpec((1, tile, d), lambda b, t: (b, t, 0)),
        pl.BlockSpec((1, SUBLANES, d), lambda b, t: (b, 0, 0)),
        pl.BlockSpec((1, SUBLANES, 3 * d), lambda b, t: (b, 0, 0)),
        pl.BlockSpec((1, n_heads, HEAD, HEAD), lambda b, t: (b, 0, 0, 0)),
    )
    scratch = [
        pltpu.VMEM((d // HEAD, SUBLANES + tile, HEAD), F32),
        pltpu.VMEM((3 * d // HEAD, SUBLANES + tile, HEAD), F32),
        pltpu.VMEM((3 * d // HEAD, tile, HEAD), F32),
        pltpu.VMEM((d // HEAD, tile, HEAD), F32),
    ]
    return pl.pallas_call(
        functools.partial(_seq_kernel, chunk=chunk),
        grid=(n, t_len // tile),
        in_specs=in_specs, out_specs=out_specs, out_shape=out_shape, scratch_shapes=scratch,
        compiler_params=pltpu.CompilerParams(dimension_semantics=("arbitrary", "arbitrary"),
                                             vmem_limit_bytes=VMEM_LIMIT_BYTES),
    )(x, ca0, cq0, s0, *weights)


SEQ_BLOCK = SUBLANES
PAIR_ROWS = 4 * SUBLANES


def _blocked_conv(prefix_ref, new, w_ref, out_ref, tail_ref, *, width, t_len, act):
    pw = (width - 1) * SEQ_BLOCK
    br = t_len * SEQ_BLOCK
    for j in range(new.shape[0] // br):
        ext = jnp.concatenate([prefix_ref[j * pw:(j + 1) * pw, :], new[j * br:(j + 1) * br, :]], axis=0)
        conv = ext[pw:pw + br] * w_ref[width - 1:width, :]
        for tap in range(width - 1):
            conv = conv + ext[tap * SEQ_BLOCK:tap * SEQ_BLOCK + br] * w_ref[tap:tap + 1, :]
        out_ref[j * br:(j + 1) * br, :] = act(conv)
        tail_ref[j * pw:(j + 1) * pw, :] = ext[br:br + pw]


def _time_slab(a, t, t_len, lanes=slice(None)):
    n_blocks = a.shape[0] // (t_len * SEQ_BLOCK)
    return jnp.concatenate([a[(j * t_len + t) * SEQ_BLOCK:(j * t_len + t + 1) * SEQ_BLOCK, lanes]
                            for j in range(n_blocks)], axis=0)


def _sample_pre_kernel(x_ref, cq0_ref, npre_ref, wq_ref, wk_ref, wv_ref, wbeta_ref, walpha_ref, cqw_ref,
                       alog_ref, dtb_ref, cq_ref, pre_ref, qkv_scr, *, t_len):
    rows, d = x_ref.shape
    n_heads = d // HEAD
    n_blocks = rows // (t_len * SEQ_BLOCK)
    n_seq = n_blocks * SEQ_BLOCK
    assert 2 * t_len == SUBLANES
    xn = _rms(x_ref[...], npre_ref[...]).astype(BF16)
    qkv_pre = jnp.concatenate([jnp.dot(xn, w[...], preferred_element_type=F32) for w in (wq_ref, wk_ref, wv_ref)],
                              axis=1)
    _blocked_conv(cq0_ref, qkv_pre, cqw_ref, qkv_scr, cq_ref, width=CONV_QKV_WIDTH, t_len=t_len, act=_silu)

    beta_full = _sigmoid(jnp.dot(xn, wbeta_ref[...], preferred_element_type=F32))
    g_full = -jnp.exp(alog_ref[...]) * _softplus(jnp.dot(xn, walpha_ref[...], preferred_element_type=F32)
                                                 + dtb_ref[...])
    beta = [_time_slab(beta_full, t, t_len) for t in range(t_len)]
    g_cum = []
    for t in range(t_len):
        g_t = _time_slab(g_full, t, t_len)
        g_cum.append(g_t if t == 0 else g_cum[-1] + g_t)
    e_g = [jnp.exp(g) for g in g_cum]
    decay = {(i, j): jnp.exp(g_cum[i] - g_cum[j]) for i in range(t_len) for j in range(i + 1)}
    k_decay = [jnp.exp(g_cum[-1] - g) for g in g_cum]
    zeros = jnp.zeros((n_seq, HEAD), F32)

    for h in range(n_heads):
        def col(a):
            return a[:, h:h + 1]
        q = [_time_slab(qkv_scr, t, t_len, slice(h * HEAD, (h + 1) * HEAD)) for t in range(t_len)]
        k = [_time_slab(qkv_scr, t, t_len, slice(d + h * HEAD, d + (h + 1) * HEAD)) for t in range(t_len)]
        v = [_time_slab(qkv_scr, t, t_len, slice(2 * d + h * HEAD, 2 * d + (h + 1) * HEAD)) for t in range(t_len)]
        q = [a * (lax.rsqrt(jnp.sum(a * a, axis=-1, keepdims=True) + EPS) * (HEAD ** -0.5)) for a in q]
        k = [a * lax.rsqrt(jnp.sum(a * a, axis=-1, keepdims=True) + EPS) for a in k]
        kb = [k[t] * col(beta[t]) for t in range(t_len)]
        a_low = {(i, j): jnp.sum(kb[i] * k[j], axis=-1, keepdims=True) * col(decay[i, j])
                 for i in range(t_len) for j in range(i)}
        attn = {(i, j): jnp.sum(q[i] * k[j], axis=-1, keepdims=True) * col(decay[i, j])
                for i in range(t_len) for j in range(i + 1)}
        t_inv = {}
        for i in range(t_len):
            for j in range(i):
                acc = a_low[i, j]
                for m in range(j + 1, i):
                    acc = acc + a_low[i, m] * t_inv[m, j]
                t_inv[i, j] = -acc
        vb = [v[t] * col(beta[t]) for t in range(t_len)]
        kbe = [kb[t] * col(e_g[t]) for t in range(t_len)]
        value, k_cum = [], []
        for i in range(t_len):
            val, kc = vb[i], kbe[i]
            for j in range(i):
                val = val + t_inv[i, j] * vb[j]
                kc = kc + t_inv[i, j] * kbe[j]
            value.append(val)
            k_cum.append(kc)
        q_eff, o_loc = [], []
        for i in range(t_len):
            qe = q[i] * col(e_g[i])
            ol = None
            for j in range(i + 1):
                qe = qe - attn[i, j] * k_cum[j]
                ol = attn[i, j] * value[j] if ol is None else ol + attn[i, j] * value[j]
            q_eff.append(qe)
            o_loc.append(ol)
        k_dec = [k[t] * col(k_decay[t]) for t in range(t_len)]
        last_decay = jnp.broadcast_to(col(e_g[-1]), (n_seq, HEAD))
        slabs = (q_eff + k_cum + o_loc + value + [zeros] * t_len + k_dec
                 + [last_decay] + [zeros] * (SUBLANES - 1))
        for group, slab in enumerate(slabs):
            for j in range(n_blocks):
                base = j * PAIR_ROWS * SEQ_BLOCK + group * SEQ_BLOCK
                pre_ref[h, base:base + SEQ_BLOCK, :] = slab[j * SEQ_BLOCK:(j + 1) * SEQ_BLOCK]


def _sample_state_kernel(pre_ref, s_ref, o_ref, s_out_ref):
    n_heads = pre_ref.shape[0]
    group = SUBLANES * SEQ_BLOCK
    row = lax.broadcasted_iota(jnp.int32, (SUBLANES, HEAD), 0)
    sign = jnp.where(row < SUBLANES // 2, 1.0, -1.0).astype(F32)

    def seq_rows(first):
        return pl.ds(first, SUBLANES, stride=SEQ_BLOCK)

    for blk in range(s_ref.shape[0] // SEQ_BLOCK):
        pre0 = blk * PAIR_ROWS * SEQ_BLOCK
        out0 = blk * group
        for i in range(SEQ_BLOCK):
            seq = blk * SEQ_BLOCK + i
            prods = [_dot(pre_ref.at[h][seq_rows(pre0 + i), :], s_ref[seq, h]) for h in range(n_heads)]
            for h in range(n_heads):
                o_vnew = pre_ref.at[h][seq_rows(pre0 + group + i), :] + prods[h] * sign
                k_dec = pre_ref.at[h][seq_rows(pre0 + 2 * group + i), :]
                last_decay = pre_ref[h, pre0 + 3 * group + i:pre0 + 3 * group + i + 1, :]
                o_ref.at[h][seq_rows(out0 + i), :] = o_vnew
                s_out_ref[seq, h] = s_ref[seq, h] * last_decay + _dot_tn(k_dec, o_vnew)


def _sample_post_kernel(x_ref, ca0_ref, o_ref, npre_ref, npost_ref, wh_ref, wb_ref, wc_ref, wz_ref, wzb_ref,
                        wga_ref, wgb_ref, bgate_ref, caw_ref, gnw_ref, wa_ref, wbo_ref, wo_ref,
                        y_ref, ca_ref, conv_scr, *, t_len):
    rows, d = x_ref.shape
    n_heads = d // HEAD
    br = t_len * SEQ_BLOCK
    n_blocks = rows // br
    x = x_ref[...]
    xn = _rms(x, npre_ref[...]).astype(BF16)

    def proj(w_ref):
        return jnp.dot(xn, w_ref[...], preferred_element_type=F32)

    _blocked_conv(ca0_ref, proj(wc_ref) * proj(wh_ref), caw_ref, conv_scr, ca_ref, width=CONV_A_WIDTH, t_len=t_len,
                  act=lambda a: a)
    y_a = proj(wb_ref) * conv_scr[...] * _silu(proj(wz_ref))
    merged = _sigmoid(proj(wga_ref) + bgate_ref[:, 0:d]) * _dot(y_a, wa_ref[...])
    group = SUBLANES * SEQ_BLOCK
    o_norm = [_rms(jnp.concatenate([o_ref[h, j * group:j * group + br, :] for j in range(n_blocks)], axis=0),
                   gnw_ref[...]) for h in range(n_heads)]
    y_b = jnp.concatenate(o_norm, axis=1) * _silu(proj(wzb_ref))
    merged = merged + _sigmoid(proj(wgb_ref) + bgate_ref[:, d:2 * d]) * _dot(y_b, wbo_ref[...])
    y_ref[...] = x + _rms(_dot(merged, wo_ref[...]), npost_ref[...])


def _to_blocks(a):
    n, t, c = a.shape
    return a.reshape(n // SEQ_BLOCK, SEQ_BLOCK, t, c).transpose(0, 2, 1, 3).reshape(n * t, c)


def _from_blocks(a, t):
    n = a.shape[0] // t
    return a.reshape(n // SEQ_BLOCK, t, SEQ_BLOCK, a.shape[-1]).transpose(0, 2, 1, 3).reshape(n, t, a.shape[-1])


def _sample_layer(x, ca0, cq0, s0, w):
    n, t_len, d = x.shape
    n_heads = d // HEAD
    n_blocks = n // SEQ_BLOCK
    tile_blocks = min(SAMPLE_TILE_BLOCKS, n_blocks)
    n_tiles = n_blocks // tile_blocks
    rows = tile_blocks * t_len * SEQ_BLOCK
    xb, cab, cqb = _to_blocks(x), _to_blocks(ca0), _to_blocks(cq0)
    wm = w["w_main"]

    def col_block(k):
        return pl.BlockSpec((d, d), lambda t: (0, k))

    def whole(a):
        nd = a.ndim
        return pl.BlockSpec(a.shape, lambda t: (0,) * nd)

    def row_tile(r, c):
        return pl.BlockSpec((r, c), lambda t: (t, 0))

    params = pltpu.CompilerParams(dimension_semantics=("arbitrary",), vmem_limit_bytes=VMEM_LIMIT_BYTES)
    pa_rows = (CONV_A_WIDTH - 1) * SEQ_BLOCK * tile_blocks
    pq_rows = (CONV_QKV_WIDTH - 1) * SEQ_BLOCK * tile_blocks
    pre_rows = PAIR_ROWS * SEQ_BLOCK

    cq_new, pre = pl.pallas_call(
        functools.partial(_sample_pre_kernel, t_len=t_len),
        grid=(n_tiles,),
        in_specs=[row_tile(rows, d), row_tile(pq_rows, 3 * d), whole(w["norm_pre"]),
                  col_block(4), col_block(5), col_block(6), whole(w["w_beta"]), whole(w["w_alpha"]),
                  whole(w["conv_qkv_w"]), whole(w["a_log0"]), whole(w["dt_bias0"])],
        out_specs=(row_tile(pq_rows, 3 * d),
                   pl.BlockSpec((n_heads, tile_blocks * pre_rows, HEAD), lambda t: (0, t, 0))),
        out_shape=(jax.ShapeDtypeStruct(cqb.shape, F32),
                   jax.ShapeDtypeStruct((n_heads, n_blocks * pre_rows, HEAD), F32)),
        scratch_shapes=[pltpu.VMEM((rows, 3 * d), F32)],
        compiler_params=params,
    )(xb, cqb, w["norm_pre"], wm, wm, wm, w["w_beta"], w["w_alpha"], w["conv_qkv_w"], w["a_log0"], w["dt_bias0"])

    o_rows = SUBLANES * SEQ_BLOCK
    step_blocks = SAMPLE_STATE_BLOCKS if n_blocks % SAMPLE_STATE_BLOCKS == 0 else 1
    step_seqs = step_blocks * SEQ_BLOCK
    o_blk, s_new = pl.pallas_call(
        _sample_state_kernel,
        grid=(n_blocks // step_blocks,),
        in_specs=[pl.BlockSpec((n_heads, step_blocks * pre_rows, HEAD), lambda j: (0, j, 0)),
                  pl.BlockSpec((step_seqs, n_heads, HEAD, HEAD), lambda j: (j, 0, 0, 0))],
        out_specs=(pl.BlockSpec((n_heads, step_blocks * o_rows, HEAD), lambda j: (0, j, 0)),
                   pl.BlockSpec((step_seqs, n_heads, HEAD, HEAD), lambda j: (j, 0, 0, 0))),
        out_shape=(jax.ShapeDtypeStruct((n_heads, n_blocks * o_rows, HEAD), F32),
                   jax.ShapeDtypeStruct(s0.shape, F32)),
        compiler_params=params,
    )(pre, s0)

    y, ca_new = pl.pallas_call(
        functools.partial(_sample_post_kernel, t_len=t_len),
        grid=(n_tiles,),
        in_specs=[row_tile(rows, d), row_tile(pa_rows, d),
                  pl.BlockSpec((n_heads, tile_blocks * o_rows, HEAD), lambda t: (0, t, 0)),
                  whole(w["norm_pre"]), whole(w["norm_post"]),
                  col_block(0), col_block(1), col_block(2), col_block(3), col_block(7), col_block(0), col_block(1),
                  whole(w["b_gate"]), whole(w["conv_a_w"]), whole(w["gnorm_w"]),
                  whole(w["w_a_out"]), whole(w["w_b_out"]), whole(w["w_o"])],
        out_specs=(row_tile(rows, d), row_tile(pa_rows, d)),
        out_shape=(jax.ShapeDtypeStruct(xb.shape, F32), jax.ShapeDtypeStruct(cab.shape, F32)),
        scratch_shapes=[pltpu.VMEM((rows, d), F32)],
        compiler_params=params,
    )(xb, cab, o_blk, w["norm_pre"], w["norm_post"], wm, wm, wm, wm, wm, w["w_gates"], w["w_gates"],
      w["b_gate"], w["conv_a_w"], w["gnorm_w"], w["w_a_out"], w["w_b_out"], w["w_o"])

    return (_from_blocks(y, t_len), _from_blocks(ca_new, CONV_A_WIDTH - 1),
            _from_blocks(cq_new, CONV_QKV_WIDTH - 1), s_new)


def _lane_row(vec, offset):
    return jnp.zeros((1, HEAD), F32).at[0, offset:offset + vec.shape[0]].set(vec.astype(F32))


def kernel(x_prompt, x_sample, state_conv_a, state_conv_qkv, state_delta, meta, norm_pre, norm_post, w_in, b_gate,
           conv_a_w, conv_qkv_w, a_log, dt_bias, gnorm_w, w_a_out, w_b_out, w_o):
    depth = w_in.shape[0]
    assert depth == 1, "single-layer trunk"
    d = x_prompt.shape[-1]
    n_heads = d // HEAD
    l = 0
    w = w_in[l]
    o_small = 8 * d
    w16 = w.astype(BF16)
    w_gates = w16[:, o_small + 2 * n_heads:]
    w_ab = jnp.concatenate([w16[:, o_small:o_small + 2 * n_heads],
                            jnp.zeros((d, HEAD - 2 * n_heads), BF16)], axis=1)
    weights = (
        norm_pre[l][None, :], norm_post[l][None, :], w16, w_gates, w_ab, b_gate[l][None, :],
        conv_a_w[l], conv_qkv_w[l], _lane_row(a_log[l], n_heads), _lane_row(dt_bias[l], n_heads),
        gnorm_w[l][None, :], w_a_out[l].astype(BF16), w_b_out[l].astype(BF16), w_o[l].astype(BF16),
    )

    zeros_a = jnp.zeros((1, SUBLANES, d), F32)
    zeros_q = jnp.zeros((1, SUBLANES, 3 * d), F32)
    zeros_s = jnp.zeros((1, n_heads, HEAD, HEAD), F32)
    _, ca_m, cq_m, s_m = _seq_layer(meta[None].astype(F32), zeros_a, zeros_q, zeros_s, weights,
                                    tile=N_META, chunk=N_META)
    y_p, ca_p, cq_p, s_p = _seq_layer(x_prompt, ca_m, cq_m, s_m, weights,
                                      tile=min(PROMPT_TILE, x_prompt.shape[1]), chunk=CHUNK)

    def small_w(cols):
        return jnp.concatenate([cols, jnp.zeros((d, HEAD - n_heads), BF16)], axis=1)

    sample_w = dict(
        norm_pre=weights[0], norm_post=weights[1], w_main=w16, w_gates=w_gates, b_gate=weights[5],
        w_beta=small_w(w16[:, o_small:o_small + n_heads]),
        w_alpha=small_w(w16[:, o_small + n_heads:o_small + 2 * n_heads]),
        conv_a_w=conv_a_w[l], conv_qkv_w=conv_qkv_w[l], a_log0=_lane_row(a_log[l], 0),
        dt_bias0=_lane_row(dt_bias[l], 0), gnorm_w=weights[10], w_a_out=weights[11], w_b_out=weights[12],
        w_o=weights[13],
    )
    y_s, ca_s, cq_s, s_s = _sample_layer(x_sample, state_conv_a[l], state_conv_qkv[l], state_delta[l], sample_w)

    def tails(c, width):
        return c[None, :, SUBLANES - (width - 1):, :]

    return (y_p, y_s, tails(ca_p, CONV_A_WIDTH), tails(cq_p, CONV_QKV_WIDTH), s_p[None],
            ca_s[None], cq_s[None], s_s[None])
```
